```python
import jax, jax.numpy as jnp
from jax import lax
import numpy as np

D_MODEL = 1024
BATCH = 1
SEQ = 16384
DEPTH = 4

MLA_HEADS = 8
QK_NOPE_DIM = 64
QK_ROPE_DIM = 32
V_HEAD_DIM = 64
Q_LORA_RANK = 384
KV_LORA_RANK = 256
ROPE_THETA = 10000.0
Q_BLOCK = 128
SG_GROUPS = 8
SG_GROUP_DIM = 64
SG_WIDTH = SG_GROUPS * SG_GROUP_DIM
SG_CHUNK = 128
CONV_WIDTH = D_MODEL
CONV_K = 3
D_FF = 2816
NORM_EPS = 1e-6

MLA_OUT = MLA_HEADS * V_HEAD_DIM
MIX_WIDTH = MLA_OUT + SG_WIDTH
QK_HEAD_DIM = QK_NOPE_DIM + QK_ROPE_DIM
EVEN_IN = Q_LORA_RANK + KV_LORA_RANK + QK_ROPE_DIM + 2 * SG_WIDTH
N_EVEN = (DEPTH + 1) // 2
N_ODD = DEPTH // 2

kernel_name = "macaron_mla_sgu_shortconv_hybrid"


def rms_norm(x, g):
    x32 = x.astype(jnp.float32)
    y = x32 * lax.rsqrt(jnp.mean(x32 * x32, axis=-1, keepdims=True) + NORM_EPS)
    return (y * g.astype(jnp.float32)).astype(x.dtype)


def swiglu(h, w_gate, w_up, w_down):
    return (jax.nn.silu(h @ w_gate) * (h @ w_up)) @ w_down


def rope_tables(positions):
    inv_freq = ROPE_THETA ** (-jnp.arange(0, QK_ROPE_DIM, 2, dtype=jnp.float32) / QK_ROPE_DIM)
    ang = positions.astype(jnp.float32)[..., None] * inv_freq
    return jnp.cos(ang), jnp.sin(ang)


def apply_rope(t, cos, sin):
    t32 = t.astype(jnp.float32)
    t1, t2 = jnp.split(t32, 2, axis=-1)
    out = jnp.concatenate([t1 * cos - t2 * sin, t1 * sin + t2 * cos], axis=-1)
    return out.astype(t.dtype)


def mla_attention(q_nope, q_rope, k_nope, k_rope, v):
    B, S, H, _ = q_nope.shape
    nb = S // Q_BLOCK
    scale = QK_HEAD_DIM ** -0.5
    k_idx = jnp.arange(S)

    def to_blocks(t):
        return jnp.moveaxis(t.reshape(B, nb, Q_BLOCK, *t.shape[2:]), 1, 0)

    def one_block(args):
        qn, qr, i = args
        s = (jnp.einsum('bqhd,bkhd->bhqk', qn, k_nope, preferred_element_type=jnp.float32)
             + jnp.einsum('bqhr,bkr->bhqk', qr, k_rope, preferred_element_type=jnp.float32)) * scale
        q_idx = i * Q_BLOCK + jnp.arange(Q_BLOCK)
        s = jnp.where(k_idx[None, :] <= q_idx[:, None], s, -jnp.inf)
        p = jax.nn.softmax(s, axis=-1).astype(v.dtype)
        return jnp.einsum('bhqk,bkhd->bqhd', p, v)

    out = lax.map(one_block, (to_blocks(q_nope), to_blocks(q_rope), jnp.arange(nb)))
    return jnp.moveaxis(out, 0, 1).reshape(B, S, H * V_HEAD_DIM)


def spatial_gating(z, sg_norm, sg_w, sg_b):
    u, v = jnp.split(z, 2, axis=-1)
    v = rms_norm(v, sg_norm)
    B, S, _ = v.shape
    nc = S // SG_CHUNK
    v = v.reshape(B, nc, SG_CHUNK, SG_GROUPS, SG_GROUP_DIM)
    w = sg_w * jnp.tril(jnp.ones((SG_CHUNK, SG_CHUNK), dtype=sg_w.dtype))
    mixed = jnp.einsum('gts,bnsgc->bntgc', w, v) + sg_b.T[None, None, :, :, None]
    return u * mixed.reshape(B, S, SG_WIDTH)


def mla_sgu_mixer(h, cos, sin, w_in, q_norm, w_uq, kv_norm, w_ukv, sg_norm, sg_w, sg_b, w_out):
    B, S, _ = h.shape
    proj = h @ w_in
    c_q, c_kv, k_rope, z = jnp.split(
        proj, [Q_LORA_RANK, Q_LORA_RANK + KV_LORA_RANK, Q_LORA_RANK + KV_LORA_RANK + QK_ROPE_DIM], axis=-1)
    q = (rms_norm(c_q, q_norm) @ w_uq).reshape(B, S, MLA_HEADS, QK_HEAD_DIM)
    q_nope = q[..., :QK_NOPE_DIM]
    q_rope = apply_rope(q[..., QK_NOPE_DIM:], cos[:, :, None, :], sin[:, :, None, :])
    k_rope = apply_rope(k_rope, cos, sin)
    kv = (rms_norm(c_kv, kv_norm) @ w_ukv).reshape(B, S, MLA_HEADS, QK_NOPE_DIM + V_HEAD_DIM)
    k_nope, v = kv[..., :QK_NOPE_DIM], kv[..., QK_NOPE_DIM:]
    attn = mla_attention(q_nope, q_rope, k_nope, k_rope, v)
    sg = spatial_gating(jax.nn.gelu(z, approximate=False), sg_norm, sg_w, sg_b)
    return jnp.concatenate([attn, sg], axis=-1) @ w_out


def short_conv_mixer(h, w_in, conv_w, w_out):
    b_gate, c_gate, z = jnp.split(h @ w_in, 3, axis=-1)
    y = lax.conv_general_dilated(
        c_gate * z, conv_w[:, None, :], window_strides=(1,), padding=[(CONV_K - 1, 0)],
        dimension_numbers=('NWC', 'WIO', 'NWC'), feature_group_count=CONV_WIDTH)
    return (b_gate * y) @ w_out


def setup_inputs(seed: int = 0) -> dict:
    key = jax.random.key(seed)
    keys = iter(jax.random.split(key, 32))
    f32 = jnp.float32

    def dense(shape, fan_in):
        return jax.random.normal(next(keys), shape, f32) * (fan_in ** -0.5)

    def gain(shape):
        return 1.0 + 0.1 * jax.random.normal(next(keys), shape, f32)

    x = jax.random.normal(next(keys), (BATCH, SEQ, D_MODEL), f32)
    offset = jax.random.randint(next(keys), (BATCH, 1), 0, 1024, dtype=jnp.int32)
    positions = offset + jnp.arange(SEQ, dtype=jnp.int32)[None, :]
    return {
        "x": x,
        "positions": positions,
        "ffn_pre_norm": gain((DEPTH, D_MODEL)),
        "ffn_pre_w_gate": dense((DEPTH, D_MODEL, D_FF), D_MODEL),
        "ffn_pre_w_up": dense((DEPTH, D_MODEL, D_FF), D_MODEL),
        "ffn_pre_w_down": dense((DEPTH, D_FF, D_MODEL), D_FF),
        "mix_norm": gain((DEPTH, D_MODEL)),
        "ffn_post_norm": gain((DEPTH, D_MODEL)),
        "ffn_post_w_gate": dense((DEPTH, D_MODEL, D_FF), D_MODEL),
        "ffn_post_w_up": dense((DEPTH, D_MODEL, D_FF), D_MODEL),
        "ffn_post_w_down": dense((DEPTH, D_FF, D_MODEL), D_FF),
        "even_w_in": dense((N_EVEN, D_MODEL, EVEN_IN), D_MODEL),
        "q_norm": gain((N_EVEN, Q_LORA_RANK)),
        "w_uq": dense((N_EVEN, Q_LORA_RANK, MLA_HEADS * QK_HEAD_DIM), Q_LORA_RANK),
        "kv_norm": gain((N_EVEN, KV_LORA_RANK)),
        "w_ukv": dense((N_EVEN, KV_LORA_RANK, MLA_HEADS * (QK_NOPE_DIM + V_HEAD_DIM)), KV_LORA_RANK),
        "sg_norm": gain((N_EVEN, SG_WIDTH)),
        "sg_w": dense((N_EVEN, SG_GROUPS, SG_CHUNK, SG_CHUNK), SG_CHUNK),
        "sg_b": gain((N_EVEN, SG_GROUPS, SG_CHUNK)),
        "even_w_out": dense((N_EVEN, MIX_WIDTH, D_MODEL), MIX_WIDTH),
        "conv_w_in": dense((N_ODD, D_MODEL, 3 * CONV_WIDTH), D_MODEL),
        "conv_w": dense((N_ODD, CONV_K, CONV_WIDTH), CONV_K),
        "conv_w_out": dense((N_ODD, CONV_WIDTH, D_MODEL), CONV_WIDTH),
        "final_norm": gain((D_MODEL,)),
    }


def reference(x, positions, ffn_pre_norm, ffn_pre_w_gate, ffn_pre_w_up, ffn_pre_w_down,
              mix_norm, ffn_post_norm, ffn_post_w_gate, ffn_post_w_up, ffn_post_w_down,
              even_w_in, q_norm, w_uq, kv_norm, w_ukv, sg_norm, sg_w, sg_b, even_w_out,
              conv_w_in, conv_w, conv_w_out, final_norm):
    cos, sin = rope_tables(positions)
    for layer in range(DEPTH):
        x = x + 0.5 * swiglu(rms_norm(x, ffn_pre_norm[layer]),
                             ffn_pre_w_gate[layer], ffn_pre_w_up[layer], ffn_pre_w_down[layer])
        h = rms_norm(x, mix_norm[layer])
        if layer % 2 == 0:
            e = layer // 2
            x = x + mla_sgu_mixer(h, cos, sin, even_w_in[e], q_norm[e], w_uq[e], kv_norm[e],
                                  w_ukv[e], sg_norm[e], sg_w[e], sg_b[e], even_w_out[e])
        else:
            o = layer // 2
            x = x + short_conv_mixer(h, conv_w_in[o], conv_w[o], conv_w_out[o])
        x = x + 0.5 * swiglu(rms_norm(x, ffn_post_norm[layer]),
                             ffn_post_w_gate[layer], ffn_post_w_up[layer], ffn_post_w_down[layer])
    return rms_norm(x, final_norm)
```

```python
import functools
import math

import jax
import jax.numpy as jnp
from jax import lax
from jax.experimental import pallas as pl
from jax.experimental.pallas import tpu as pltpu

D_MODEL = 1024
DEPTH = 4
MLA_HEADS = 8
QK_NOPE_DIM = 64
QK_ROPE_DIM = 32
V_HEAD_DIM = 64
Q_LORA_RANK = 384
KV_LORA_RANK = 256
ROPE_THETA = 10000.0
SG_GROUPS = 8
SG_GROUP_DIM = 64
SG_WIDTH = SG_GROUPS * SG_GROUP_DIM
SG_CHUNK = 128
CONV_K = 3
D_FF = 2816
NORM_EPS = 1e-6
QK_HEAD_DIM = QK_NOPE_DIM + QK_ROPE_DIM
MLA_OUT = MLA_HEADS * V_HEAD_DIM
EVEN_IN = Q_LORA_RANK + KV_LORA_RANK + QK_ROPE_DIM + 2 * SG_WIDTH
HALF_ROPE = QK_ROPE_DIM // 2

V7X_LANES = 128
V7X_MXU_COLS = 256
V7X_VMEM_BYTES = 64 * 1024 * 1024

HEAD_PAD = V7X_LANES
FFN_TM = 512
FFN_TF = V7X_MXU_COLS
PROJ_TM = 512
ATT_TQ = 512
ATT_TK = 256
VMEM_LIMIT = 56 * 1024 * 1024

_NT = (((1,), (1,)), ((), ()))
_TN = (((0,), (0,)), ((), ()))
_F32 = jnp.float32
_BF16 = jnp.bfloat16


def _dot(a, b, dims=None):
    if dims is None:
        return jnp.dot(a, b, preferred_element_type=_F32)
    return lax.dot_general(a, b, dims, preferred_element_type=_F32)


def _rms_rows(x, g):
    ms = jnp.mean(x * x, axis=-1, keepdims=True)
    return x * lax.rsqrt(ms + NORM_EPS) * g


def _rms_cols(xT, g):
    ms = jnp.mean(xT * xT, axis=0, keepdims=True)
    return xT * lax.rsqrt(ms + NORM_EPS) * g


def _const_spec(shape):
    n = len(shape)
    return pl.BlockSpec(shape, lambda *_: (0,) * n, pipeline_mode=pl.Buffered(1))


def _params(semantics):
    return pltpu.CompilerParams(dimension_semantics=semantics,
                                vmem_limit_bytes=VMEM_LIMIT)


def _ffn_body(has_mix, has_final, *refs):
    it = iter(refs)
    x_ref = next(it)
    if has_mix:
        attn_ref, sg_ref, woa_ref, wos_ref = next(it), next(it), next(it), next(it)
    g_ref, wg_ref, wu_ref, wd_ref = next(it), next(it), next(it), next(it)
    gf_ref = next(it) if has_final else None
    o_ref, h_ref, a_ref = next(it), next(it), next(it)

    x = x_ref[...]
    if has_mix:
        x = x + _dot(attn_ref[...], woa_ref[...], _TN) + _dot(sg_ref[...], wos_ref[...], _TN)
    h_ref[...] = _rms_rows(x, g_ref[...]).astype(_BF16)
    for j in range(D_FF // FFN_TF):
        cols = slice(j * FFN_TF, (j + 1) * FFN_TF)
        gate = _dot(h_ref[...], wg_ref[:, cols])
        up = _dot(h_ref[...], wu_ref[:, cols])
        a_ref[:, cols] = (gate * jax.nn.sigmoid(gate) * up).astype(_BF16)
    out = x + 0.5 * _dot(a_ref[...], wd_ref[...])
    if has_final:
        out = _rms_rows(out, gf_ref[...])
    o_ref[...] = out


def _ffn(x, g, wg, wu, wd, mix=None, final_g=None):
    S = x.shape[0]
    tm = min(FFN_TM, S)
    row_spec = pl.BlockSpec((tm, D_MODEL), lambda i: (i, 0))
    args, specs = [x], [row_spec]
    if mix is not None:
        attnT, sgT, woa, wos = mix
        colT = pl.BlockSpec((MLA_OUT, tm), lambda i: (0, i))
        args += [attnT, sgT, woa, wos]
        specs += [colT, colT, _const_spec(woa.shape), _const_spec(wos.shape)]
    args += [g, wg, wu, wd]
    specs += [_const_spec(g.shape), _const_spec(wg.shape), _const_spec(wu.shape),
              _const_spec(wd.shape)]
    if final_g is not None:
        args.append(final_g)
        specs.append(_const_spec(final_g.shape))
    return pl.pallas_call(
        functools.partial(_ffn_body, mix is not None, final_g is not None),
        out_shape=jax.ShapeDtypeStruct((S, D_MODEL), _F32),
        grid=(S // tm,),
        in_specs=specs,
        out_specs=row_spec,
        scratch_shapes=[pltpu.VMEM((tm, D_MODEL), _BF16), pltpu.VMEM((tm, D_FF), _BF16)],
        compiler_params=_params(("arbitrary",)),
        name="ffn_mix" if mix is not None else ("ffn_final" if final_g is not None else "ffn"),
    )(*args)


def _even_proj_body(x_ref, g_ref, winT_ref, qn_ref, wuqT_ref, kvn_ref, wuvT_ref,
                    wk_ref, sgn_ref, sgwT_ref, sgb_ref, cos_ref, sin_ref,
                    qT_ref, k_ref, vT_ref, sgT_ref):
    tm = x_ref.shape[0]
    h = _rms_rows(x_ref[...], g_ref[...]).astype(_BF16)
    projT = _dot(winT_ref[...], h, _NT)
    o_kv = Q_LORA_RANK
    o_kr = o_kv + KV_LORA_RANK
    o_z = o_kr + QK_ROPE_DIM
    cos = cos_ref[...]
    sin = sin_ref[...]

    cqn = _rms_cols(projT[0:o_kv], qn_ref[...]).astype(_BF16)
    qT = _dot(wuqT_ref[...], cqn) * (QK_HEAD_DIM ** -0.5 * math.log2(math.e))
    pieces = []
    for hd in range(MLA_HEADS):
        b = hd * HEAD_PAD
        t1 = qT[b + QK_NOPE_DIM:b + QK_NOPE_DIM + HALF_ROPE]
        t2 = qT[b + QK_NOPE_DIM + HALF_ROPE:b + QK_HEAD_DIM]
        pieces += [qT[b:b + QK_NOPE_DIM], t1 * cos - t2 * sin, t1 * sin + t2 * cos,
                   qT[b + QK_HEAD_DIM:b + HEAD_PAD]]
    qT_ref[...] = jnp.concatenate(pieces, axis=0).astype(_BF16)

    ckvn = _rms_cols(projT[o_kv:o_kr], kvn_ref[...]).astype(_BF16)
    vT_ref[...] = _dot(wuvT_ref[...], ckvn).astype(_BF16)
    k1 = projT[o_kr:o_kr + HALF_ROPE]
    k2 = projT[o_kr + HALF_ROPE:o_z]
    kro = jnp.concatenate([k1 * cos - k2 * sin, k1 * sin + k2 * cos], axis=0)
    lhsT = jnp.concatenate([ckvn, kro.astype(_BF16)], axis=0)
    k_all = _dot(lhsT, wk_ref[...], _TN).astype(_BF16)
    for hd in range(MLA_HEADS):
        k_ref[hd] = k_all[:, hd * HEAD_PAD:(hd + 1) * HEAD_PAD]

    z = projT[o_z:EVEN_IN]
    z = 0.5 * z * (1.0 + lax.erf(z * (2.0 ** -0.5)))
    uT = z[0:SG_WIDTH]
    vn = _rms_cols(z[SG_WIDTH:2 * SG_WIDTH], sgn_ref[...]).astype(_BF16)
    nchunk = tm // SG_CHUNK
    s_idx = lax.broadcasted_iota(jnp.int32, (SG_CHUNK, SG_CHUNK), 0)
    t_idx = lax.broadcasted_iota(jnp.int32, (SG_CHUNK, SG_CHUNK), 1)
    causal = s_idx <= t_idx
    rows = []
    for gi in range(SG_GROUPS):
        r0 = gi * SG_GROUP_DIM
        lhs = jnp.concatenate(
            [vn[r0:r0 + SG_GROUP_DIM, n * SG_CHUNK:(n + 1) * SG_CHUNK] for n in range(nchunk)],
            axis=0)
        wT = jnp.where(causal, sgwT_ref[gi], 0.0).astype(_BF16)
        mixed = _dot(lhs, wT) + sgb_ref[gi:gi + 1, :]
        rows.append(jnp.concatenate(
            [mixed[n * SG_GROUP_DIM:(n + 1) * SG_GROUP_DIM] for n in range(nchunk)], axis=1))
    sgT_ref[...] = (uT * jnp.concatenate(rows, axis=0)).astype(_BF16)


def _even_proj(x, g, winT, qn, wuqT, kvn, wuvT, wk, sgn, sgwT, sgb, cosT, sinT):
    S = x.shape[0]
    tm = min(PROJ_TM, S)
    colT = lambda rows: pl.BlockSpec((rows, tm), lambda i: (0, i))
    consts = [g, winT, qn, wuqT, kvn, wuvT, wk, sgn, sgwT, sgb]
    return pl.pallas_call(
        _even_proj_body,
        out_shape=(jax.ShapeDtypeStruct((MLA_HEADS * HEAD_PAD, S), _BF16),
                   jax.ShapeDtypeStruct((MLA_HEADS, S, HEAD_PAD), _BF16),
                   jax.ShapeDtypeStruct((MLA_OUT, S), _BF16),
                   jax.ShapeDtypeStruct((SG_WIDTH, S), _BF16)),
        grid=(S // tm,),
        in_specs=[pl.BlockSpec((tm, D_MODEL), lambda i: (i, 0))]
        + [_const_spec(c.shape) for c in consts]
        + [colT(HALF_ROPE), colT(HALF_ROPE)],
        out_specs=(colT(MLA_HEADS * HEAD_PAD),
                   pl.BlockSpec((MLA_HEADS, tm, HEAD_PAD), lambda i: (0, i, 0)),
                   colT(MLA_OUT), colT(SG_WIDTH)),
        compiler_params=_params(("arbitrary",)),
        name="even_proj",
    )(x, *consts, cosT, sinT)


def _attn_body(qT_ref, k_ref, vT_ref, o_ref):
    tq = qT_ref.shape[1]
    i = pl.program_id(1)
    qT = qT_ref[...]

    def tile(j, carry, masked):
        m, l, acc = carry
        k0 = pl.multiple_of(j * ATT_TK, ATT_TK)
        s = _dot(k_ref[pl.ds(k0, ATT_TK), :], qT)
        if masked:
            kidx = k0 + lax.broadcasted_iota(jnp.int32, (ATT_TK, tq), 0)
            qidx = i * tq + lax.broadcasted_iota(jnp.int32, (ATT_TK, tq), 1)
            s = jnp.where(kidx <= qidx, s, -jnp.inf)
        m_new = jnp.maximum(m, jnp.max(s, axis=0, keepdims=True))
        alpha = jnp.exp2(m - m_new)
        p = jnp.exp2(s - m_new)
        l = alpha * l + jnp.sum(p, axis=0, keepdims=True)
        acc = alpha * acc + _dot(vT_ref[:, pl.ds(k0, ATT_TK)], p.astype(_BF16))
        return m_new, l, acc

    init = (jnp.full((1, tq), -jnp.inf, _F32), jnp.zeros((1, tq), _F32),
            jnp.zeros((V_HEAD_DIM, tq), _F32))
    n_full = i * (tq // ATT_TK)
    carry = lax.fori_loop(0, n_full, lambda j, c: tile(j, c, False), init)
    for d in range(tq // ATT_TK):
        carry = tile(n_full + d, carry, True)
    _, l, acc = carry
    o_ref[...] = (acc / l).astype(_BF16)


def _attention(qT, k, vT):
    S = qT.shape[1]
    tq = min(ATT_TQ, S)
    return pl.pallas_call(
        _attn_body,
        out_shape=jax.ShapeDtypeStruct((MLA_OUT, S), _BF16),
        grid=(MLA_HEADS, S // tq),
        in_specs=[pl.BlockSpec((HEAD_PAD, tq), lambda h, i: (h, i)),
                  pl.BlockSpec((None, S, HEAD_PAD), lambda h, i: (h, 0, 0)),
                  pl.BlockSpec((V_HEAD_DIM, S), lambda h, i: (h, 0))],
        out_specs=pl.BlockSpec((V_HEAD_DIM, tq), lambda h, i: (h, i)),
        compiler_params=_params(("arbitrary", "arbitrary")),
        name="mla_attention",
    )(qT, k, vT)


CONV_HALO = 8


def _conv_body(x_ref, g_ref, win_ref, cw_ref, wout_ref, o_ref, cz_ref):
    tm = x_ref.shape[0]
    x = x_ref[...]
    h = _rms_rows(x, g_ref[...]).astype(_BF16)
    W = D_MODEL
    b_gate = _dot(h, win_ref[:, 0:W])
    cz = _dot(h, win_ref[:, W:2 * W]) * _dot(h, win_ref[:, 2 * W:3 * W])

    @pl.when(pl.program_id(0) == 0)
    def _():
        cz_ref[0:CONV_HALO, :] = jnp.zeros((CONV_HALO, W), _F32)

    cz_ref[CONV_HALO:CONV_HALO + tm, :] = cz
    y = cz * cw_ref[2:3, :]
    y = y + cz_ref[CONV_HALO - 1:CONV_HALO - 1 + tm, :] * cw_ref[1:2, :]
    y = y + cz_ref[CONV_HALO - 2:CONV_HALO - 2 + tm, :] * cw_ref[0:1, :]
    cz_ref[0:CONV_HALO, :] = cz[tm - CONV_HALO:tm]
    o_ref[...] = x + _dot((b_gate * y).astype(_BF16), wout_ref[...])


def _conv_mixer(x, g, win, cw, wout):
    S = x.shape[0]
    tm = min(PROJ_TM, S)
    row_spec = pl.BlockSpec((tm, D_MODEL), lambda i: (i, 0))
    return pl.pallas_call(
        _conv_body,
        out_shape=jax.ShapeDtypeStruct((S, D_MODEL), _F32),
        grid=(S // tm,),
        in_specs=[row_spec, _const_spec(g.shape), _const_spec(win.shape),
                  _const_spec(cw.shape), _const_spec(wout.shape)],
        out_specs=row_spec,
        scratch_shapes=[pltpu.VMEM((tm + CONV_HALO, D_MODEL), _F32)],
        compiler_params=_params(("arbitrary",)),
        name="conv_mixer",
    )(x, g, win, cw, wout)


def _prep_even(w_in, w_uq, w_ukv, sg_w):
    winT = w_in.T.astype(_BF16)
    wq = w_uq.reshape(Q_LORA_RANK, MLA_HEADS, QK_HEAD_DIM)
    wq = jnp.pad(wq, ((0, 0), (0, 0), (0, HEAD_PAD - QK_HEAD_DIM)))
    wuqT = wq.reshape(Q_LORA_RANK, MLA_HEADS * HEAD_PAD).T.astype(_BF16)
    wkv = w_ukv.reshape(KV_LORA_RANK, MLA_HEADS, QK_NOPE_DIM + V_HEAD_DIM)
    wuvT = wkv[:, :, QK_NOPE_DIM:].reshape(KV_LORA_RANK, MLA_OUT).T.astype(_BF16)
    wk_nope = jnp.pad(wkv[:, :, :QK_NOPE_DIM], ((0, 0), (0, 0), (0, HEAD_PAD - QK_NOPE_DIM)))
    place = jnp.pad(jnp.eye(QK_ROPE_DIM, dtype=_F32),
                    ((0, 0), (QK_NOPE_DIM, HEAD_PAD - QK_HEAD_DIM)))
    place = jnp.broadcast_to(place[:, None, :], (QK_ROPE_DIM, MLA_HEADS, HEAD_PAD))
    wk = jnp.concatenate([wk_nope, place], axis=0).reshape(
        KV_LORA_RANK + QK_ROPE_DIM, MLA_HEADS * HEAD_PAD).astype(_BF16)
    sgwT = jnp.swapaxes(sg_w, 1, 2)
    return winT, wuqT, wuvT, wk, sgwT


def kernel(x, positions, ffn_pre_norm, ffn_pre_w_gate, ffn_pre_w_up, ffn_pre_w_down, mix_norm, ffn_post_norm, ffn_post_w_gate, ffn_post_w_up, ffn_post_w_down, even_w_in, q_norm, w_uq, kv_norm, w_ukv, sg_norm, sg_w, sg_b, even_w_out, conv_w_in, conv_w, conv_w_out, final_norm):
    B, S, _ = x.shape
    assert B == 1 and S % SG_CHUNK == 0
    xs = x[0]
    inv_freq = ROPE_THETA ** (-jnp.arange(0, QK_ROPE_DIM, 2, dtype=_F32) / QK_ROPE_DIM)
    ang = positions[0].astype(_F32)[None, :] * inv_freq[:, None]
    cosT, sinT = jnp.cos(ang), jnp.sin(ang)
    bf = lambda w: w.astype(_BF16)
    row = lambda v: v.reshape(1, -1)
    col = lambda v: v.reshape(-1, 1)

    for layer in range(DEPTH):
        xs = _ffn(xs, row(ffn_pre_norm[layer]), bf(ffn_pre_w_gate[layer]),
                  bf(ffn_pre_w_up[layer]), bf(ffn_pre_w_down[layer]))
        mix = None
        if layer % 2 == 0:
            e = layer // 2
            winT, wuqT, wuvT, wk, sgwT = _prep_even(even_w_in[e], w_uq[e], w_ukv[e], sg_w[e])
            qT, k, vT, sgT = _even_proj(
                xs, row(mix_norm[layer]), winT, col(q_norm[e]), wuqT, col(kv_norm[e]), wuvT, wk,
                col(sg_norm[e]), sgwT, sg_b[e], cosT, sinT)
            attnT = _attention(qT, k, vT)
            wo = bf(even_w_out[e])
            mix = (attnT, sgT, wo[:MLA_OUT], wo[MLA_OUT:])
        else:
            o = layer // 2
            xs = _conv_mixer(xs, row(mix_norm[layer]), bf(conv_w_in[o]), conv_w[o],
                             bf(conv_w_out[o]))
        xs = _ffn(xs, row(ffn_post_norm[layer]), bf(ffn_post_w_gate[layer]),
                  bf(ffn_post_w_up[layer]), bf(ffn_post_w_down[layer]), mix=mix,
                  final_g=row(final_norm) if layer == DEPTH - 1 else None)
    return xs[None]
```

```python
import functools
import math

import jax
import jax.numpy as jnp
from jax import lax
from jax.experimental import pallas as pl
from jax.experimental.pallas import tpu as pltpu

D_MODEL = 1024
DEPTH = 4
MLA_HEADS = 8
QK_NOPE_DIM = 64
QK_ROPE_DIM = 32
V_HEAD_DIM = 64
Q_LORA_RANK = 384
KV_LORA_RANK = 256
ROPE_THETA = 10000.0
SG_GROUPS = 8
SG_GROUP_DIM = 64
SG_WIDTH = SG_GROUPS * SG_GROUP_DIM
SG_CHUNK = 128
CONV_K = 3
D_FF = 2816
NORM_EPS = 1e-6
QK_HEAD_DIM = QK_NOPE_DIM + QK_ROPE_DIM
MLA_OUT = MLA_HEADS * V_HEAD_DIM
EVEN_IN = Q_LORA_RANK + KV_LORA_RANK + QK_ROPE_DIM + 2 * SG_WIDTH
HALF_ROPE = QK_ROPE_DIM // 2

V7X_LANES = 128
V7X_MXU_COLS = 256
V7X_VMEM_BYTES = 64 * 1024 * 1024

HEAD_PAD = V7X_LANES
FFN_TM = 512
FFN_TF = V7X_MXU_COLS
PROJ_TM = 512
ATT_TQ = 1024
VMEM_LIMIT = 56 * 1024 * 1024

_NT = (((1,), (1,)), ((), ()))
_TN = (((0,), (0,)), ((), ()))
_F32 = jnp.float32
_BF16 = jnp.bfloat16


def _dot(a, b, dims=None):
    if dims is None:
        return jnp.dot(a, b, preferred_element_type=_F32)
    return lax.dot_general(a, b, dims, preferred_element_type=_F32)


def _rms_rows(x, g):
    ms = jnp.mean(x * x, axis=-1, keepdims=True)
    return x * lax.rsqrt(ms + NORM_EPS) * g


def _rms_cols(xT, g):
    ms = jnp.mean(xT * xT, axis=0, keepdims=True)
    return xT * lax.rsqrt(ms + NORM_EPS) * g


def _const_spec(shape):
    n = len(shape)
    return pl.BlockSpec(shape, lambda *_: (0,) * n, pipeline_mode=pl.Buffered(1))


def _params(semantics, flags=None):
    return pltpu.CompilerParams(dimension_semantics=semantics,
                                vmem_limit_bytes=VMEM_LIMIT, flags=flags)


def _ffn_body(has_mix, has_final, *refs):
    it = iter(refs)
    x_ref = next(it)
    if has_mix:
        attn_ref, sg_ref, woa_ref, wos_ref = next(it), next(it), next(it), next(it)
    g_ref, wg_ref, wu_ref, wd_ref = next(it), next(it), next(it), next(it)
    gf_ref = next(it) if has_final else None
    o_ref, h_ref, a_ref = next(it), next(it), next(it)

    x = x_ref[...]
    if has_mix:
        x = x + _dot(attn_ref[...], woa_ref[...], _TN) + _dot(sg_ref[...], wos_ref[...], _TN)
    h_ref[...] = _rms_rows(x, g_ref[...]).astype(_BF16)
    for j in range(D_FF // FFN_TF):
        cols = slice(j * FFN_TF, (j + 1) * FFN_TF)
        gate = _dot(h_ref[...], wg_ref[:, cols])
        up = _dot(h_ref[...], wu_ref[:, cols])
        a_ref[:, cols] = (gate * jax.nn.sigmoid(gate) * up).astype(_BF16)
    out = x + 0.5 * _dot(a_ref[...], wd_ref[...])
    if has_final:
        out = _rms_rows(out, gf_ref[...])
    o_ref[...] = out


def _ffn(x, g, wg, wu, wd, mix=None, final_g=None):
    S = x.shape[0]
    tm = min(FFN_TM, S)
    row_spec = pl.BlockSpec((tm, D_MODEL), lambda i: (i, 0))
    args, specs = [x], [row_spec]
    if mix is not None:
        attnT, sgT, woa, wos = mix
        colT = pl.BlockSpec((MLA_OUT, tm), lambda i: (0, i))
        args += [attnT, sgT, woa, wos]
        specs += [colT, colT, _const_spec(woa.shape), _const_spec(wos.shape)]
    args += [g, wg, wu, wd]
    specs += [_const_spec(g.shape), _const_spec(wg.shape), _const_spec(wu.shape),
              _const_spec(wd.shape)]
    if final_g is not None:
        args.append(final_g)
        specs.append(_const_spec(final_g.shape))
    return pl.pallas_call(
        functools.partial(_ffn_body, mix is not None, final_g is not None),
        out_shape=jax.ShapeDtypeStruct((S, D_MODEL), _F32),
        grid=(S // tm,),
        in_specs=specs,
        out_specs=row_spec,
        scratch_shapes=[pltpu.VMEM((tm, D_MODEL), _BF16), pltpu.VMEM((tm, D_FF), _BF16)],
        compiler_params=_params(("arbitrary",)),
        name="ffn_mix" if mix is not None else ("ffn_final" if final_g is not None else "ffn"),
    )(*args)


def _even_proj_body(x_ref, g_ref, winT_ref, qn_ref, wuqT_ref, kvn_ref, wuvT_ref,
                    wk_ref, sgn_ref, sgwT_ref, sgb_ref, cos_ref, sin_ref,
                    qT_ref, k_ref, vT_ref, sgT_ref):
    tm = x_ref.shape[0]
    h = _rms_rows(x_ref[...], g_ref[...]).astype(_BF16)
    projT = _dot(winT_ref[...], h, _NT)
    o_kv = Q_LORA_RANK
    o_kr = o_kv + KV_LORA_RANK
    o_z = o_kr + QK_ROPE_DIM
    cos = cos_ref[...]
    sin = sin_ref[...]

    cqn = _rms_cols(projT[0:o_kv], qn_ref[...]).astype(_BF16)
    qT = _dot(wuqT_ref[...], cqn) * (QK_HEAD_DIM ** -0.5 * math.log2(math.e))
    pieces = []
    for hd in range(MLA_HEADS):
        b = hd * HEAD_PAD
        t1 = qT[b + QK_NOPE_DIM:b + QK_NOPE_DIM + HALF_ROPE]
        t2 = qT[b + QK_NOPE_DIM + HALF_ROPE:b + QK_HEAD_DIM]
        pieces += [qT[b:b + QK_NOPE_DIM], t1 * cos - t2 * sin, t1 * sin + t2 * cos,
                   qT[b + QK_HEAD_DIM:b + HEAD_PAD]]
    qT_ref[...] = jnp.concatenate(pieces, axis=0).astype(_BF16)

    ckvn = _rms_cols(projT[o_kv:o_kr], kvn_ref[...]).astype(_BF16)
    vT_ref[...] = _dot(wuvT_ref[...], ckvn).astype(_BF16)
    k1 = projT[o_kr:o_kr + HALF_ROPE]
    k2 = projT[o_kr + HALF_ROPE:o_z]
    kro = jnp.concatenate([k1 * cos - k2 * sin, k1 * sin + k2 * cos], axis=0)
    lhsT = jnp.concatenate([ckvn, kro.astype(_BF16)], axis=0)
    k_all = _dot(lhsT, wk_ref[...], _TN).astype(_BF16)
    for hd in range(MLA_HEADS):
        k_ref[hd] = k_all[:, hd * HEAD_PAD:(hd + 1) * HEAD_PAD]

    z = projT[o_z:EVEN_IN]
    z = 0.5 * z * (1.0 + lax.erf(z * (2.0 ** -0.5)))
    uT = z[0:SG_WIDTH]
    vn = _rms_cols(z[SG_WIDTH:2 * SG_WIDTH], sgn_ref[...]).astype(_BF16)
    nchunk = tm // SG_CHUNK
    s_idx = lax.broadcasted_iota(jnp.int32, (SG_CHUNK, SG_CHUNK), 0)
    t_idx = lax.broadcasted_iota(jnp.int32, (SG_CHUNK, SG_CHUNK), 1)
    causal = s_idx <= t_idx
    rows = []
    for gi in range(SG_GROUPS):
        r0 = gi * SG_GROUP_DIM
        lhs = jnp.concatenate(
            [vn[r0:r0 + SG_GROUP_DIM, n * SG_CHUNK:(n + 1) * SG_CHUNK] for n in range(nchunk)],
            axis=0)
        wT = jnp.where(causal, sgwT_ref[gi], 0.0).astype(_BF16)
        mixed = _dot(lhs, wT) + sgb_ref[gi:gi + 1, :]
        rows.append(jnp.concatenate(
            [mixed[n * SG_GROUP_DIM:(n + 1) * SG_GROUP_DIM] for n in range(nchunk)], axis=1))
    sgT_ref[...] = (uT * jnp.concatenate(rows, axis=0)).astype(_BF16)


def _even_proj(x, g, winT, qn, wuqT, kvn, wuvT, wk, sgn, sgwT, sgb, cosT, sinT):
    S = x.shape[0]
    tm = min(PROJ_TM, S)
    colT = lambda rows: pl.BlockSpec((rows, tm), lambda i: (0, i))
    consts = [g, winT, qn, wuqT, kvn, wuvT, wk, sgn, sgwT, sgb]
    return pl.pallas_call(
        _even_proj_body,
        out_shape=(jax.ShapeDtypeStruct((MLA_HEADS * HEAD_PAD, S), _BF16),
                   jax.ShapeDtypeStruct((MLA_HEADS, S, HEAD_PAD), _BF16),
                   jax.ShapeDtypeStruct((MLA_OUT, S), _BF16),
                   jax.ShapeDtypeStruct((SG_WIDTH, S), _BF16)),
        grid=(S // tm,),
        in_specs=[pl.BlockSpec((tm, D_MODEL), lambda i: (i, 0))]
        + [_const_spec(c.shape) for c in consts]
        + [colT(HALF_ROPE), colT(HALF_ROPE)],
        out_specs=(colT(MLA_HEADS * HEAD_PAD),
                   pl.BlockSpec((MLA_HEADS, tm, HEAD_PAD), lambda i: (0, i, 0)),
                   colT(MLA_OUT), colT(SG_WIDTH)),
        compiler_params=_params(("arbitrary",)),
        name="even_proj",
    )(x, *consts, cosT, sinT)


ONES_ROWS = 16


def _attn_body(qT_ref, k_ref, vT_ref, o_ref, s0_ref, s1_ref, c0_ref, c1_ref,
               p0_ref, p1_ref, a0_ref, a1_ref, m_ref, acc_ref):
    tq = qT_ref.shape[1]
    tk = tq // 2
    i = pl.program_id(1)

    def stage_a(j, s_ref, c_ref, masked):
        k0 = pl.multiple_of(j * tk, tk)
        s = _dot(k_ref[pl.ds(k0, tk), :], qT_ref[...])
        if masked:
            kidx = j * tk + lax.broadcasted_iota(jnp.int32, (tk, tq), 0)
            qidx = i * tq + lax.broadcasted_iota(jnp.int32, (tk, tq), 1)
            s = jnp.where(kidx <= qidx, s, -jnp.inf)
        s_ref[...] = s
        c_ref[...] = jnp.max(s, axis=0, keepdims=True)

    def stage_b(s_ref, c_ref, p_ref, a_ref):
        m_old = m_ref[...]
        m_new = jnp.maximum(m_old, c_ref[...])
        m_ref[...] = m_new
        a_ref[...] = jnp.exp2(m_old - m_new)
        p_ref[...] = jnp.exp2(s_ref[...] - m_new).astype(_BF16)

    def stage_c(j, p_ref, a_ref):
        k0 = pl.multiple_of(j * tk, tk)
        v1 = jnp.concatenate([vT_ref[:, pl.ds(k0, tk)], jnp.ones((ONES_ROWS, tk), _BF16)], axis=0)
        acc_ref[...] = a_ref[...] * acc_ref[...] + _dot(v1, p_ref[...])

    def pair(u, next_masked, last):
        stage_a(2 * u + 1, s1_ref, c1_ref, last)
        stage_c(jnp.maximum(2 * u - 1, 0), p1_ref, a1_ref)
        stage_b(s0_ref, c0_ref, p0_ref, a0_ref)
        if not last:
            stage_a(2 * u + 2, s0_ref, c0_ref, next_masked)
        stage_c(2 * u, p0_ref, a0_ref)
        stage_b(s1_ref, c1_ref, p1_ref, a1_ref)

    p1_ref[...] = jnp.zeros(p1_ref.shape, _BF16)
    a1_ref[...] = jnp.ones(a1_ref.shape, _F32)
    acc_ref[...] = jnp.zeros(acc_ref.shape, _F32)
    m_ref[...] = jnp.full(m_ref.shape, -jnp.inf, _F32)
    stage_a(0, s0_ref, c0_ref, True)

    def body(u, carry):
        pair(u, False, False)
        return carry

    lax.fori_loop(0, i - 1, body, 0)

    @pl.when(i > 0)
    def _():
        pair(i - 1, True, False)

    pair(i, False, True)
    stage_c(2 * i + 1, p1_ref, a1_ref)
    acc = acc_ref[...]
    o_ref[...] = (acc[0:V_HEAD_DIM] / acc[V_HEAD_DIM:V_HEAD_DIM + 1]).astype(_BF16)


def _attention(qT, k, vT):
    S = qT.shape[1]
    tq = min(ATT_TQ, S)
    tk = tq // 2
    return pl.pallas_call(
        _attn_body,
        out_shape=jax.ShapeDtypeStruct((MLA_OUT, S), _BF16),
        grid=(MLA_HEADS, S // tq),
        in_specs=[pl.BlockSpec((HEAD_PAD, tq), lambda h, i: (h, i)),
                  pl.BlockSpec((None, S, HEAD_PAD), lambda h, i: (h, 0, 0)),
                  pl.BlockSpec((V_HEAD_DIM, S), lambda h, i: (h, 0))],
        out_specs=pl.BlockSpec((V_HEAD_DIM, tq), lambda h, i: (h, i)),
        scratch_shapes=[pltpu.VMEM((tk, tq), _F32), pltpu.VMEM((tk, tq), _F32),
                        pltpu.VMEM((1, tq), _F32), pltpu.VMEM((1, tq), _F32),
                        pltpu.VMEM((tk, tq), _BF16), pltpu.VMEM((tk, tq), _BF16),
                        pltpu.VMEM((1, tq), _F32), pltpu.VMEM((1, tq), _F32),
                        pltpu.VMEM((1, tq), _F32),
                        pltpu.VMEM((V_HEAD_DIM + ONES_ROWS, tq), _F32)],
        compiler_params=_params(("arbitrary", "arbitrary")),
        name="mla_attention",
    )(qT, k, vT)


CONV_HALO = 8


def _conv_body(x_ref, g_ref, win_ref, cw_ref, wout_ref, o_ref, cz_ref):
    tm = x_ref.shape[0]
    x = x_ref[...]
    h = _rms_rows(x, g_ref[...]).astype(_BF16)
    W = D_MODEL
    b_gate = _dot(h, win_ref[:, 0:W])
    cz = _dot(h, win_ref[:, W:2 * W]) * _dot(h, win_ref[:, 2 * W:3 * W])

    @pl.when(pl.program_id(0) == 0)
    def _():
        cz_ref[0:CONV_HALO, :] = jnp.zeros((CONV_HALO, W), _F32)

    cz_ref[CONV_HALO:CONV_HALO + tm, :] = cz
    y = cz * cw_ref[2:3, :]
    y = y + cz_ref[CONV_HALO - 1:CONV_HALO - 1 + tm, :] * cw_ref[1:2, :]
    y = y + cz_ref[CONV_HALO - 2:CONV_HALO - 2 + tm, :] * cw_ref[0:1, :]
    cz_ref[0:CONV_HALO, :] = cz[tm - CONV_HALO:tm]
    o_ref[...] = x + _dot((b_gate * y).astype(_BF16), wout_ref[...])


def _conv_mixer(x, g, win, cw, wout):
    S = x.shape[0]
    tm = min(PROJ_TM, S)
    row_spec = pl.BlockSpec((tm, D_MODEL), lambda i: (i, 0))
    return pl.pallas_call(
        _conv_body,
        out_shape=jax.ShapeDtypeStruct((S, D_MODEL), _F32),
        grid=(S // tm,),
        in_specs=[row_spec, _const_spec(g.shape), _const_spec(win.shape),
                  _const_spec(cw.shape), _const_spec(wout.shape)],
        out_specs=row_spec,
        scratch_shapes=[pltpu.VMEM((tm + CONV_HALO, D_MODEL), _F32)],
        compiler_params=_params(("arbitrary",)),
        name="conv_mixer",
    )(x, g, win, cw, wout)


def _prep_even(w_in, w_uq, w_ukv, sg_w):
    winT = w_in.T.astype(_BF16)
    wq = w_uq.reshape(Q_LORA_RANK, MLA_HEADS, QK_HEAD_DIM)
    wq = jnp.pad(wq, ((0, 0), (0, 0), (0, HEAD_PAD - QK_HEAD_DIM)))
    wuqT = wq.reshape(Q_LORA_RANK, MLA_HEADS * HEAD_PAD).T.astype(_BF16)
    wkv = w_ukv.reshape(KV_LORA_RANK, MLA_HEADS, QK_NOPE_DIM + V_HEAD_DIM)
    wuvT = wkv[:, :, QK_NOPE_DIM:].reshape(KV_LORA_RANK, MLA_OUT).T.astype(_BF16)
    wk_nope = jnp.pad(wkv[:, :, :QK_NOPE_DIM], ((0, 0), (0, 0), (0, HEAD_PAD - QK_NOPE_DIM)))
    place = jnp.pad(jnp.eye(QK_ROPE_DIM, dtype=_F32),
                    ((0, 0), (QK_NOPE_DIM, HEAD_PAD - QK_HEAD_DIM)))
    place = jnp.broadcast_to(place[:, None, :], (QK_ROPE_DIM, MLA_HEADS, HEAD_PAD))
    wk = jnp.concatenate([wk_nope, place], axis=0).reshape(
        KV_LORA_RANK + QK_ROPE_DIM, MLA_HEADS * HEAD_PAD).astype(_BF16)
    sgwT = jnp.swapaxes(sg_w, 1, 2)
    return winT, wuqT, wuvT, wk, sgwT


def kernel(x, positions, ffn_pre_norm, ffn_pre_w_gate, ffn_pre_w_up, ffn_pre_w_down, mix_norm, ffn_post_norm, ffn_post_w_gate, ffn_post_w_up, ffn_post_w_down, even_w_in, q_norm, w_uq, kv_norm, w_ukv, sg_norm, sg_w, sg_b, even_w_out, conv_w_in, conv_w, conv_w_out, final_norm):
    B, S, _ = x.shape
    assert B == 1 and S % SG_CHUNK == 0
    xs = x[0]
    inv_freq = ROPE_THETA ** (-jnp.arange(0, QK_ROPE_DIM, 2, dtype=_F32) / QK_ROPE_DIM)
    ang = positions[0].astype(_F32)[None, :] * inv_freq[:, None]
    cosT, sinT = jnp.cos(ang), jnp.sin(ang)
    bf = lambda w: w.astype(_BF16)
    row = lambda v: v.reshape(1, -1)
    col = lambda v: v.reshape(-1, 1)

    for layer in range(DEPTH):
        xs = _ffn(xs, row(ffn_pre_norm[layer]), bf(ffn_pre_w_gate[layer]),
                  bf(ffn_pre_w_up[layer]), bf(ffn_pre_w_down[layer]))
        mix = None
        if layer % 2 == 0:
            e = layer // 2
            winT, wuqT, wuvT, wk, sgwT = _prep_even(even_w_in[e], w_uq[e], w_ukv[e], sg_w[e])
            qT, k, vT, sgT = _even_proj(
                xs, row(mix_norm[layer]), winT, col(q_norm[e]), wuqT, col(kv_norm[e]), wuvT, wk,
                col(sg_norm[e]), sgwT, sg_b[e], cosT, sinT)
            attnT = _attention(qT, k, vT)
            wo = bf(even_w_out[e])
            mix = (attnT, sgT, wo[:MLA_OUT], wo[MLA_OUT:])
        else:
            o = layer // 2
            xs = _conv_mixer(xs, row(mix_norm[layer]), bf(conv_w_in[o]), conv_w[o],
                             bf(conv_w_out[o]))
        xs = _ffn(xs, row(ffn_post_norm[layer]), bf(ffn_post_w_gate[layer]),
                  bf(ffn_post_w_up[layer]), bf(ffn_post_w_down[layer]), mix=mix,
                  final_g=row(final_norm) if layer == DEPTH - 1 else None)
    return xs[None]
```

```python
import functools
import math

import jax
import jax.numpy as jnp
from jax import lax
from jax.experimental import pallas as pl
from jax.experimental.pallas import tpu as pltpu

D_MODEL = 1024
DEPTH = 4
MLA_HEADS = 8
QK_NOPE_DIM = 64
QK_ROPE_DIM = 32
V_HEAD_DIM = 64
Q_LORA_RANK = 384
KV_LORA_RANK = 256
ROPE_THETA = 10000.0
SG_GROUPS = 8
SG_GROUP_DIM = 64
SG_WIDTH = SG_GROUPS * SG_GROUP_DIM
SG_CHUNK = 128
CONV_K = 3
D_FF = 2816
NORM_EPS = 1e-6
QK_HEAD_DIM = QK_NOPE_DIM + QK_ROPE_DIM
MLA_OUT = MLA_HEADS * V_HEAD_DIM
EVEN_IN = Q_LORA_RANK + KV_LORA_RANK + QK_ROPE_DIM + 2 * SG_WIDTH
HALF_ROPE = QK_ROPE_DIM // 2

V7X_LANES = 128
V7X_MXU_COLS = 256
V7X_VMEM_BYTES = 64 * 1024 * 1024

HEAD_PAD = V7X_LANES
FFN_TM = 512
FFN_TF = V7X_MXU_COLS
PROJ_TM = 512
ATT_TQ = 1024
VMEM_LIMIT = 56 * 1024 * 1024

_NT = (((1,), (1,)), ((), ()))
_TN = (((0,), (0,)), ((), ()))
_F32 = jnp.float32
_BF16 = jnp.bfloat16


def _dot(a, b, dims=None):
    if dims is None:
        return jnp.dot(a, b, preferred_element_type=_F32)
    return lax.dot_general(a, b, dims, preferred_element_type=_F32)


def _rms_rows(x, g):
    ms = jnp.mean(x * x, axis=-1, keepdims=True)
    return x * lax.rsqrt(ms + NORM_EPS) * g


def _rms_cols(xT, g):
    ms = jnp.mean(xT * xT, axis=0, keepdims=True)
    return xT * lax.rsqrt(ms + NORM_EPS) * g


def _const_spec(shape):
    n = len(shape)
    return pl.BlockSpec(shape, lambda *_: (0,) * n, pipeline_mode=pl.Buffered(1))


def _params(semantics, flags=None):
    return pltpu.CompilerParams(dimension_semantics=semantics,
                                vmem_limit_bytes=VMEM_LIMIT, flags=flags)


def _ffn_body(has_mix, has_final, *refs):
    it = iter(refs)
    x_ref = next(it)
    if has_mix:
        attn_ref, sg_ref, woa_ref, wos_ref = next(it), next(it), next(it), next(it)
    g_ref, wg_ref, wu_ref, wd_ref = next(it), next(it), next(it), next(it)
    gf_ref = next(it) if has_final else None
    o_ref, h_ref, a_ref = next(it), next(it), next(it)

    x = x_ref[...]
    if has_mix:
        x = x + _dot(attn_ref[...], woa_ref[...], _TN) + _dot(sg_ref[...], wos_ref[...], _TN)
    h_ref[...] = _rms_rows(x, g_ref[...]).astype(_BF16)
    for j in range(D_FF // FFN_TF):
        cols = slice(j * FFN_TF, (j + 1) * FFN_TF)
        gate = _dot(h_ref[...], wg_ref[:, cols])
        up = _dot(h_ref[...], wu_ref[:, cols])
        a_ref[:, cols] = (gate * jax.nn.sigmoid(gate) * up).astype(_BF16)
    out = x + 0.5 * _dot(a_ref[...], wd_ref[...])
    if has_final:
        out = _rms_rows(out, gf_ref[...])
    o_ref[...] = out


def _ffn(x, g, wg, wu, wd, mix=None, final_g=None):
    S = x.shape[0]
    tm = min(FFN_TM, S)
    row_spec = pl.BlockSpec((tm, D_MODEL), lambda i: (i, 0))
    args, specs = [x], [row_spec]
    if mix is not None:
        attnT, sgT, woa, wos = mix
        colT = pl.BlockSpec((MLA_OUT, tm), lambda i: (0, i))
        args += [attnT, sgT, woa, wos]
        specs += [colT, colT, _const_spec(woa.shape), _const_spec(wos.shape)]
    args += [g, wg, wu, wd]
    specs += [_const_spec(g.shape), _const_spec(wg.shape), _const_spec(wu.shape),
              _const_spec(wd.shape)]
    if final_g is not None:
        args.append(final_g)
        specs.append(_const_spec(final_g.shape))
    return pl.pallas_call(
        functools.partial(_ffn_body, mix is not None, final_g is not None),
        out_shape=jax.ShapeDtypeStruct((S, D_MODEL), _F32),
        grid=(S // tm,),
        in_specs=specs,
        out_specs=row_spec,
        scratch_shapes=[pltpu.VMEM((tm, D_MODEL), _BF16), pltpu.VMEM((tm, D_FF), _BF16)],
        compiler_params=_params(("arbitrary",)),
        name="ffn_mix" if mix is not None else ("ffn_final" if final_g is not None else "ffn"),
    )(*args)


def _even_proj_body(x_ref, g_ref, winT_ref, qn_ref, wuqT_ref, kvn_ref, wuvT_ref,
                    wk_ref, sgn_ref, sgwT_ref, sgb_ref, cos_ref, sin_ref,
                    qT_ref, k_ref, vT_ref, sgT_ref):
    tm = x_ref.shape[0]
    h = _rms_rows(x_ref[...], g_ref[...]).astype(_BF16)
    projT = _dot(winT_ref[...], h, _NT)
    o_kv = Q_LORA_RANK
    o_kr = o_kv + KV_LORA_RANK
    o_z = o_kr + QK_ROPE_DIM
    cos = cos_ref[...]
    sin = sin_ref[...]

    cqn = _rms_cols(projT[0:o_kv], qn_ref[...]).astype(_BF16)
    qT = _dot(wuqT_ref[...], cqn) * (QK_HEAD_DIM ** -0.5 * math.log2(math.e))
    pieces = []
    for hd in range(MLA_HEADS):
        b = hd * HEAD_PAD
        t1 = qT[b + QK_NOPE_DIM:b + QK_NOPE_DIM + HALF_ROPE]
        t2 = qT[b + QK_NOPE_DIM + HALF_ROPE:b + QK_HEAD_DIM]
        pieces += [qT[b:b + QK_NOPE_DIM], t1 * cos - t2 * sin, t1 * sin + t2 * cos,
                   qT[b + QK_HEAD_DIM:b + HEAD_PAD]]
    qT_ref[...] = jnp.concatenate(pieces, axis=0).astype(_BF16)

    ckvn = _rms_cols(projT[o_kv:o_kr], kvn_ref[...]).astype(_BF16)
    vT_ref[...] = _dot(wuvT_ref[...], ckvn).astype(_BF16)
    k1 = projT[o_kr:o_kr + HALF_ROPE]
    k2 = projT[o_kr + HALF_ROPE:o_z]
    kro = jnp.concatenate([k1 * cos - k2 * sin, k1 * sin + k2 * cos], axis=0)
    lhsT = jnp.concatenate([ckvn, kro.astype(_BF16)], axis=0)
    k_all = _dot(lhsT, wk_ref[...], _TN).astype(_BF16)
    for hd in range(MLA_HEADS):
        k_ref[hd] = k_all[:, hd * HEAD_PAD:(hd + 1) * HEAD_PAD]

    z = projT[o_z:EVEN_IN]
    z = 0.5 * z * (1.0 + lax.erf(z * (2.0 ** -0.5)))
    uT = z[0:SG_WIDTH]
    vn = _rms_cols(z[SG_WIDTH:2 * SG_WIDTH], sgn_ref[...]).astype(_BF16)
    nchunk = tm // SG_CHUNK
    s_idx = lax.broadcasted_iota(jnp.int32, (SG_CHUNK, SG_CHUNK), 0)
    t_idx = lax.broadcasted_iota(jnp.int32, (SG_CHUNK, SG_CHUNK), 1)
    causal = s_idx <= t_idx
    rows = []
    for gi in range(SG_GROUPS):
        r0 = gi * SG_GROUP_DIM
        lhs = jnp.concatenate(
            [vn[r0:r0 + SG_GROUP_DIM, n * SG_CHUNK:(n + 1) * SG_CHUNK] for n in range(nchunk)],
            axis=0)
        wT = jnp.where(causal, sgwT_ref[gi], 0.0).astype(_BF16)
        mixed = _dot(lhs, wT) + sgb_ref[gi:gi + 1, :]
        rows.append(jnp.concatenate(
            [mixed[n * SG_GROUP_DIM:(n + 1) * SG_GROUP_DIM] for n in range(nchunk)], axis=1))
    sgT_ref[...] = (uT * jnp.concatenate(rows, axis=0)).astype(_BF16)


def _even_proj(x, g, winT, qn, wuqT, kvn, wuvT, wk, sgn, sgwT, sgb, cosT, sinT):
    S = x.shape[0]
    tm = min(PROJ_TM, S)
    colT = lambda rows: pl.BlockSpec((rows, tm), lambda i: (0, i))
    consts = [g, winT, qn, wuqT, kvn, wuvT, wk, sgn, sgwT, sgb]
    return pl.pallas_call(
        _even_proj_body,
        out_shape=(jax.ShapeDtypeStruct((MLA_HEADS * HEAD_PAD, S), _BF16),
                   jax.ShapeDtypeStruct((MLA_HEADS, S, HEAD_PAD), _BF16),
                   jax.ShapeDtypeStruct((MLA_OUT, S), _BF16),
                   jax.ShapeDtypeStruct((SG_WIDTH, S), _BF16)),
        grid=(S // tm,),
        in_specs=[pl.BlockSpec((tm, D_MODEL), lambda i: (i, 0))]
        + [_const_spec(c.shape) for c in consts]
        + [colT(HALF_ROPE), colT(HALF_ROPE)],
        out_specs=(colT(MLA_HEADS * HEAD_PAD),
                   pl.BlockSpec((MLA_HEADS, tm, HEAD_PAD), lambda i: (0, i, 0)),
                   colT(MLA_OUT), colT(SG_WIDTH)),
        compiler_params=_params(("arbitrary",)),
        name="even_proj",
    )(x, *consts, cosT, sinT)


ONES_ROWS = 16


def _attn_body(qT_ref, k_ref, vT_ref, o_ref, s0_ref, s1_ref, c0_ref, c1_ref,
               p0_ref, p1_ref, a0_ref, a1_ref, m_ref, acc_ref):
    tq = qT_ref.shape[1]
    tk = tq // 2
    i = pl.program_id(1)

    def stage_a(j, s_ref, c_ref, masked):
        k0 = pl.multiple_of(j * tk, tk)
        s = _dot(k_ref[pl.ds(k0, tk), :], qT_ref[...])
        if masked:
            kidx = j * tk + lax.broadcasted_iota(jnp.int32, (tk, tq), 0)
            qidx = i * tq + lax.broadcasted_iota(jnp.int32, (tk, tq), 1)
            s = jnp.where(kidx <= qidx, s, -jnp.inf)
        s_ref[...] = s
        c_ref[...] = jnp.max(s, axis=0, keepdims=True)

    def stage_b(s_ref, c_ref, p_ref, a_ref):
        m_old = m_ref[...]
        m_new = jnp.maximum(m_old, c_ref[...])
        m_ref[...] = m_new
        a_ref[...] = jnp.exp2(m_old - m_new)
        p_ref[...] = jnp.exp2(s_ref[...] - m_new).astype(_BF16)

    def stage_c(j, p_ref, a_ref):
        k0 = pl.multiple_of(j * tk, tk)
        v1 = jnp.concatenate([vT_ref[:, pl.ds(k0, tk)], jnp.ones((ONES_ROWS, tk), _BF16)], axis=0)
        acc_ref[...] = a_ref[...] * acc_ref[...] + _dot(v1, p_ref[...])

    def pair(u, a_masked):
        for d, (s_ref, c_ref, p_ref, a_ref) in enumerate(
                ((s0_ref, c0_ref, p0_ref, a0_ref), (s1_ref, c1_ref, p1_ref, a1_ref))):
            stage_c(jnp.maximum(2 * u - 2 + d, 0), p_ref, a_ref)
            stage_b(s_ref, c_ref, p_ref, a_ref)
            if a_masked is not None:
                stage_a(2 * u + 2 + d, s_ref, c_ref, a_masked)

    p0_ref[...] = jnp.zeros(p0_ref.shape, _BF16)
    p1_ref[...] = jnp.zeros(p1_ref.shape, _BF16)
    a0_ref[...] = jnp.ones(a0_ref.shape, _F32)
    a1_ref[...] = jnp.ones(a1_ref.shape, _F32)
    acc_ref[...] = jnp.zeros(acc_ref.shape, _F32)
    m_ref[...] = jnp.full(m_ref.shape, -jnp.inf, _F32)
    stage_a(0, s0_ref, c0_ref, True)
    stage_a(1, s1_ref, c1_ref, True)

    def body(u, carry):
        pair(u, False)
        return carry

    lax.fori_loop(0, i - 1, body, 0)

    @pl.when(i > 0)
    def _():
        pair(i - 1, True)

    pair(i, None)
    stage_c(2 * i, p0_ref, a0_ref)
    stage_c(2 * i + 1, p1_ref, a1_ref)
    acc = acc_ref[...]
    o_ref[...] = (acc[0:V_HEAD_DIM] / acc[V_HEAD_DIM:V_HEAD_DIM + 1]).astype(_BF16)


def _attention(qT, k, vT):
    S = qT.shape[1]
    tq = min(ATT_TQ, S)
    tk = tq // 2
    return pl.pallas_call(
        _attn_body,
        out_shape=jax.ShapeDtypeStruct((MLA_OUT, S), _BF16),
        grid=(MLA_HEADS, S // tq),
        in_specs=[pl.BlockSpec((HEAD_PAD, tq), lambda h, i: (h, i)),
                  pl.BlockSpec((None, S, HEAD_PAD), lambda h, i: (h, 0, 0)),
                  pl.BlockSpec((V_HEAD_DIM, S), lambda h, i: (h, 0))],
        out_specs=pl.BlockSpec((V_HEAD_DIM, tq), lambda h, i: (h, i)),
        scratch_shapes=[pltpu.VMEM((tk, tq), _F32), pltpu.VMEM((tk, tq), _F32),
                        pltpu.VMEM((1, tq), _F32), pltpu.VMEM((1, tq), _F32),
                        pltpu.VMEM((tk, tq), _BF16), pltpu.VMEM((tk, tq), _BF16),
                        pltpu.VMEM((1, tq), _F32), pltpu.VMEM((1, tq), _F32),
                        pltpu.VMEM((1, tq), _F32),
                        pltpu.VMEM((V_HEAD_DIM + ONES_ROWS, tq), _F32)],
        compiler_params=_params(("arbitrary", "arbitrary")),
        name="mla_attention",
    )(qT, k, vT)


CONV_HALO = 8


def _conv_body(x_ref, g_ref, win_ref, cw_ref, wout_ref, o_ref, cz_ref):
    tm = x_ref.shape[0]
    x = x_ref[...]
    h = _rms_rows(x, g_ref[...]).astype(_BF16)
    W = D_MODEL
    b_gate = _dot(h, win_ref[:, 0:W])
    cz = _dot(h, win_ref[:, W:2 * W]) * _dot(h, win_ref[:, 2 * W:3 * W])

    @pl.when(pl.program_id(0) == 0)
    def _():
        cz_ref[0:CONV_HALO, :] = jnp.zeros((CONV_HALO, W), _F32)

    cz_ref[CONV_HALO:CONV_HALO + tm, :] = cz
    y = cz * cw_ref[2:3, :]
    y = y + cz_ref[CONV_HALO - 1:CONV_HALO - 1 + tm, :] * cw_ref[1:2, :]
    y = y + cz_ref[CONV_HALO - 2:CONV_HALO - 2 + tm, :] * cw_ref[0:1, :]
    cz_ref[0:CONV_HALO, :] = cz[tm - CONV_HALO:tm]
    o_ref[...] = x + _dot((b_gate * y).astype(_BF16), wout_ref[...])


def _conv_mixer(x, g, win, cw, wout):
    S = x.shape[0]
    tm = min(PROJ_TM, S)
    row_spec = pl.BlockSpec((tm, D_MODEL), lambda i: (i, 0))
    return pl.pallas_call(
        _conv_body,
        out_shape=jax.ShapeDtypeStruct((S, D_MODEL), _F32),
        grid=(S // tm,),
        in_specs=[row_spec, _const_spec(g.shape), _const_spec(win.shape),
                  _const_spec(cw.shape), _const_spec(wout.shape)],
        out_specs=row_spec,
        scratch_shapes=[pltpu.VMEM((tm + CONV_HALO, D_MODEL), _F32)],
        compiler_params=_params(("arbitrary",)),
        name="conv_mixer",
    )(x, g, win, cw, wout)


def _prep_even(w_in, w_uq, w_ukv, sg_w):
    winT = w_in.T.astype(_BF16)
    wq = w_uq.reshape(Q_LORA_RANK, MLA_HEADS, QK_HEAD_DIM)
    wq = jnp.pad(wq, ((0, 0), (0, 0), (0, HEAD_PAD - QK_HEAD_DIM)))
    wuqT = wq.reshape(Q_LORA_RANK, MLA_HEADS * HEAD_PAD).T.astype(_BF16)
    wkv = w_ukv.reshape(KV_LORA_RANK, MLA_HEADS, QK_NOPE_DIM + V_HEAD_DIM)
    wuvT = wkv[:, :, QK_NOPE_DIM:].reshape(KV_LORA_RANK, MLA_OUT).T.astype(_BF16)
    wk_nope = jnp.pad(wkv[:, :, :QK_NOPE_DIM], ((0, 0), (0, 0), (0, HEAD_PAD - QK_NOPE_DIM)))
    place = jnp.pad(jnp.eye(QK_ROPE_DIM, dtype=_F32),
                    ((0, 0), (QK_NOPE_DIM, HEAD_PAD - QK_HEAD_DIM)))
    place = jnp.broadcast_to(place[:, None, :], (QK_ROPE_DIM, MLA_HEADS, HEAD_PAD))
    wk = jnp.concatenate([wk_nope, place], axis=0).reshape(
        KV_LORA_RANK + QK_ROPE_DIM, MLA_HEADS * HEAD_PAD).astype(_BF16)
    sgwT = jnp.swapaxes(sg_w, 1, 2)
    return winT, wuqT, wuvT, wk, sgwT


def kernel(x, positions, ffn_pre_norm, ffn_pre_w_gate, ffn_pre_w_up, ffn_pre_w_down, mix_norm, ffn_post_norm, ffn_post_w_gate, ffn_post_w_up, ffn_post_w_down, even_w_in, q_norm, w_uq, kv_norm, w_ukv, sg_norm, sg_w, sg_b, even_w_out, conv_w_in, conv_w, conv_w_out, final_norm):
    B, S, _ = x.shape
    assert B == 1 and S % SG_CHUNK == 0
    xs = x[0]
    inv_freq = ROPE_THETA ** (-jnp.arange(0, QK_ROPE_DIM, 2, dtype=_F32) / QK_ROPE_DIM)
    ang = positions[0].astype(_F32)[None, :] * inv_freq[:, None]
    cosT, sinT = jnp.cos(ang), jnp.sin(ang)
    bf = lambda w: w.astype(_BF16)
    row = lambda v: v.reshape(1, -1)
    col = lambda v: v.reshape(-1, 1)

    for layer in range(DEPTH):
        xs = _ffn(xs, row(ffn_pre_norm[layer]), bf(ffn_pre_w_gate[layer]),
                  bf(ffn_pre_w_up[layer]), bf(ffn_pre_w_down[layer]))
        mix = None
        if layer % 2 == 0:
            e = layer // 2
            winT, wuqT, wuvT, wk, sgwT = _prep_even(even_w_in[e], w_uq[e], w_ukv[e], sg_w[e])
            qT, k, vT, sgT = _even_proj(
                xs, row(mix_norm[layer]), winT, col(q_norm[e]), wuqT, col(kv_norm[e]), wuvT, wk,
                col(sg_norm[e]), sgwT, sg_b[e], cosT, sinT)
            attnT = _attention(qT, k, vT)
            wo = bf(even_w_out[e])
            mix = (attnT, sgT, wo[:MLA_OUT], wo[MLA_OUT:])
        else:
            o = layer // 2
            xs = _conv_mixer(xs, row(mix_norm[layer]), bf(conv_w_in[o]), conv_w[o],
                             bf(conv_w_out[o]))
        xs = _ffn(xs, row(ffn_post_norm[layer]), bf(ffn_post_w_gate[layer]),
                  bf(ffn_post_w_up[layer]), bf(ffn_post_w_down[layer]), mix=mix,
                  final_g=row(final_norm) if layer == DEPTH - 1 else None)
    return xs[None]
```

```python
import functools
import math

import jax
import jax.numpy as jnp
from jax import lax
from jax.experimental import pallas as pl
from jax.experimental.pallas import tpu as pltpu

D_MODEL = 1024
DEPTH = 4
MLA_HEADS = 8
QK_NOPE_DIM = 64
QK_ROPE_DIM = 32
V_HEAD_DIM = 64
Q_LORA_RANK = 384
KV_LORA_RANK = 256
ROPE_THETA = 10000.0
SG_GROUPS = 8
SG_GROUP_DIM = 64
SG_WIDTH = SG_GROUPS * SG_GROUP_DIM
SG_CHUNK = 128
CONV_K = 3
D_FF = 2816
NORM_EPS = 1e-6
QK_HEAD_DIM = QK_NOPE_DIM + QK_ROPE_DIM
MLA_OUT = MLA_HEADS * V_HEAD_DIM
EVEN_IN = Q_LORA_RANK + KV_LORA_RANK + QK_ROPE_DIM + 2 * SG_WIDTH
HALF_ROPE = QK_ROPE_DIM // 2

V7X_LANES = 128
V7X_MXU_COLS = 256
V7X_VMEM_BYTES = 64 * 1024 * 1024

HEAD_PAD = V7X_LANES
FFN_TM = 512
FFN_TF = V7X_MXU_COLS
PROJ_TM = 512
ATT_TQ = 1024
VMEM_LIMIT = 56 * 1024 * 1024

_NT = (((1,), (1,)), ((), ()))
_TN = (((0,), (0,)), ((), ()))
_F32 = jnp.float32
_BF16 = jnp.bfloat16


def _dot(a, b, dims=None):
    if dims is None:
        return jnp.dot(a, b, preferred_element_type=_F32)
    return lax.dot_general(a, b, dims, preferred_element_type=_F32)


def _rms_rows(x, g):
    ms = jnp.mean(x * x, axis=-1, keepdims=True)
    return x * lax.rsqrt(ms + NORM_EPS) * g


def _rms_cols(xT, g):
    ms = jnp.mean(xT * xT, axis=0, keepdims=True)
    return xT * lax.rsqrt(ms + NORM_EPS) * g


def _const_spec(shape):
    n = len(shape)
    return pl.BlockSpec(shape, lambda *_: (0,) * n, pipeline_mode=pl.Buffered(1))


def _params(semantics, flags=None):
    return pltpu.CompilerParams(dimension_semantics=semantics,
                                vmem_limit_bytes=VMEM_LIMIT, flags=flags)


def _ffn_body(has_mix, has_final, *refs):
    it = iter(refs)
    x_ref = next(it)
    if has_mix:
        attn_ref, sg_ref, woa_ref, wos_ref = next(it), next(it), next(it), next(it)
    g_ref, wg_ref, wu_ref, wd_ref = next(it), next(it), next(it), next(it)
    gf_ref = next(it) if has_final else None
    o_ref, h_ref, a_ref = next(it), next(it), next(it)

    x = x_ref[...]
    if has_mix:
        x = x + _dot(attn_ref[...], woa_ref[...], _TN) + _dot(sg_ref[...], wos_ref[...], _TN)
    h_ref[...] = _rms_rows(x, g_ref[...]).astype(_BF16)
    for j in range(D_FF // FFN_TF):
        cols = slice(j * FFN_TF, (j + 1) * FFN_TF)
        gate = _dot(h_ref[...], wg_ref[:, cols])
        up = _dot(h_ref[...], wu_ref[:, cols])
        a_ref[:, cols] = (gate * jax.nn.sigmoid(gate) * up).astype(_BF16)
    out = x + 0.5 * _dot(a_ref[...], wd_ref[...])
    if has_final:
        out = _rms_rows(out, gf_ref[...])
    o_ref[...] = out


def _ffn(x, g, wg, wu, wd, mix=None, final_g=None):
    S = x.shape[0]
    tm = min(FFN_TM, S)
    row_spec = pl.BlockSpec((tm, D_MODEL), lambda i: (i, 0))
    args, specs = [x], [row_spec]
    if mix is not None:
        attnT, sgT, woa, wos = mix
        colT = pl.BlockSpec((MLA_OUT, tm), lambda i: (0, i))
        args += [attnT, sgT, woa, wos]
        specs += [colT, colT, _const_spec(woa.shape), _const_spec(wos.shape)]
    args += [g, wg, wu, wd]
    specs += [_const_spec(g.shape), _const_spec(wg.shape), _const_spec(wu.shape),
              _const_spec(wd.shape)]
    if final_g is not None:
        args.append(final_g)
        specs.append(_const_spec(final_g.shape))
    return pl.pallas_call(
        functools.partial(_ffn_body, mix is not None, final_g is not None),
        out_shape=jax.ShapeDtypeStruct((S, D_MODEL), _F32),
        grid=(S // tm,),
        in_specs=specs,
        out_specs=row_spec,
        scratch_shapes=[pltpu.VMEM((tm, D_MODEL), _BF16), pltpu.VMEM((tm, D_FF), _BF16)],
        compiler_params=_params(("arbitrary",)),
        name="ffn_mix" if mix is not None else ("ffn_final" if final_g is not None else "ffn"),
    )(*args)


def _even_proj_body(x_ref, g_ref, winT_ref, qn_ref, wuqT_ref, kvn_ref, wuvT_ref,
                    wk_ref, sgn_ref, sgwT_ref, sgb_ref, cos_ref, sin_ref,
                    qT_ref, k_ref, vT_ref, sgT_ref):
    tm = x_ref.shape[0]
    h = _rms_rows(x_ref[...], g_ref[...]).astype(_BF16)
    projT = _dot(winT_ref[...], h, _NT)
    o_kv = Q_LORA_RANK
    o_kr = o_kv + KV_LORA_RANK
    o_z = o_kr + QK_ROPE_DIM
    cos = cos_ref[...]
    sin = sin_ref[...]

    cqn = _rms_cols(projT[0:o_kv], qn_ref[...]).astype(_BF16)
    qT = _dot(wuqT_ref[...], cqn) * (QK_HEAD_DIM ** -0.5 * math.log2(math.e))
    pieces = []
    for hd in range(MLA_HEADS):
        b = hd * HEAD_PAD
        t1 = qT[b + QK_NOPE_DIM:b + QK_NOPE_DIM + HALF_ROPE]
        t2 = qT[b + QK_NOPE_DIM + HALF_ROPE:b + QK_HEAD_DIM]
        pieces += [qT[b:b + QK_NOPE_DIM], t1 * cos - t2 * sin, t1 * sin + t2 * cos,
                   qT[b + QK_HEAD_DIM:b + HEAD_PAD]]
    qT_ref[...] = jnp.concatenate(pieces, axis=0).astype(_BF16)

    ckvn = _rms_cols(projT[o_kv:o_kr], kvn_ref[...]).astype(_BF16)
    vT_ref[...] = _dot(wuvT_ref[...], ckvn).astype(_BF16)
    k1 = projT[o_kr:o_kr + HALF_ROPE]
    k2 = projT[o_kr + HALF_ROPE:o_z]
    kro = jnp.concatenate([k1 * cos - k2 * sin, k1 * sin + k2 * cos], axis=0)
    lhsT = jnp.concatenate([ckvn, kro.astype(_BF16)], axis=0)
    k_all = _dot(lhsT, wk_ref[...], _TN).astype(_BF16)
    for hd in range(MLA_HEADS):
        k_ref[hd] = k_all[:, hd * HEAD_PAD:(hd + 1) * HEAD_PAD]

    z = projT[o_z:EVEN_IN]
    z = 0.5 * z * (1.0 + lax.erf(z * (2.0 ** -0.5)))
    uT = z[0:SG_WIDTH]
    vn = _rms_cols(z[SG_WIDTH:2 * SG_WIDTH], sgn_ref[...]).astype(_BF16)
    nchunk = tm // SG_CHUNK
    s_idx = lax.broadcasted_iota(jnp.int32, (SG_CHUNK, SG_CHUNK), 0)
    t_idx = lax.broadcasted_iota(jnp.int32, (SG_CHUNK, SG_CHUNK), 1)
    causal = s_idx <= t_idx
    rows = []
    for gi in range(SG_GROUPS):
        r0 = gi * SG_GROUP_DIM
        lhs = jnp.concatenate(
            [vn[r0:r0 + SG_GROUP_DIM, n * SG_CHUNK:(n + 1) * SG_CHUNK] for n in range(nchunk)],
            axis=0)
        wT = jnp.where(causal, sgwT_ref[gi], 0.0).astype(_BF16)
        mixed = _dot(lhs, wT) + sgb_ref[gi:gi + 1, :]
        rows.append(jnp.concatenate(
            [mixed[n * SG_GROUP_DIM:(n + 1) * SG_GROUP_DIM] for n in range(nchunk)], axis=1))
    sgT_ref[...] = (uT * jnp.concatenate(rows, axis=0)).astype(_BF16)


def _even_proj(x, g, winT, qn, wuqT, kvn, wuvT, wk, sgn, sgwT, sgb, cosT, sinT):
    S = x.shape[0]
    tm = min(PROJ_TM, S)
    colT = lambda rows: pl.BlockSpec((rows, tm), lambda i: (0, i))
    consts = [g, winT, qn, wuqT, kvn, wuvT, wk, sgn, sgwT, sgb]
    return pl.pallas_call(
        _even_proj_body,
        out_shape=(jax.ShapeDtypeStruct((MLA_HEADS * HEAD_PAD, S), _BF16),
                   jax.ShapeDtypeStruct((MLA_HEADS, S, HEAD_PAD), _BF16),
                   jax.ShapeDtypeStruct((MLA_OUT, S), _BF16),
                   jax.ShapeDtypeStruct((SG_WIDTH, S), _BF16)),
        grid=(S // tm,),
        in_specs=[pl.BlockSpec((tm, D_MODEL), lambda i: (i, 0))]
        + [_const_spec(c.shape) for c in consts]
        + [colT(HALF_ROPE), colT(HALF_ROPE)],
        out_specs=(colT(MLA_HEADS * HEAD_PAD),
                   pl.BlockSpec((MLA_HEADS, tm, HEAD_PAD), lambda i: (0, i, 0)),
                   colT(MLA_OUT), colT(SG_WIDTH)),
        compiler_params=_params(("arbitrary",)),
        name="even_proj",
    )(x, *consts, cosT, sinT)


ONES_ROWS = 16


ATT_LEAD = 2
ATT_UNROLL = 8


def _attn_tables(nq):
    below = [(iq, jk, 0) for iq in range(nq) for jk in range(2 * iq)]
    diag = [(iq, 2 * iq + d, 1 + d) for iq in range(nq) for d in range(2)]
    rows, seqs = [], []
    for seq in (below, diag):
        if seq:
            seqs.append((len(rows) + ATT_LEAD, len(seq)))
            rows += [(0, 0, 0)] * ATT_LEAD + seq + [seq[-1]] * ATT_LEAD
        else:
            seqs.append((0, 0))
    return tuple(jnp.asarray([r[c] for r in rows], jnp.int32) for c in range(3)), seqs


def _attn_body(iq_ref, jk_ref, sel_ref, qT_ref, k_ref, vT_ref, o_ref,
               s0_ref, s1_ref, c0_ref, c1_ref, p0_ref, p1_ref, a0_ref, a1_ref,
               m_ref, acc_ref, bias_ref, *, seqs):
    tk, tq = s0_ref.shape
    nq = acc_ref.shape[0]
    bufs = ((s0_ref, c0_ref, p0_ref, a0_ref), (s1_ref, c1_ref, p1_ref, a1_ref))

    @pl.when(pl.program_id(0) == 0)
    def _():
        r = lax.broadcasted_iota(jnp.int32, (tk, tq), 0)
        q = lax.broadcasted_iota(jnp.int32, (tk, tq), 1)
        bias_ref[0] = jnp.where(r <= q, 0.0, -jnp.inf)
        bias_ref[1] = jnp.where(r + tk <= q, 0.0, -jnp.inf)

    def stage_a(e, s_ref, c_ref, masked):
        k0 = pl.multiple_of(jk_ref[e] * tk, tk)
        q0 = pl.multiple_of(iq_ref[e] * tq, tq)
        s = _dot(k_ref[pl.ds(k0, tk), :], qT_ref[:, pl.ds(q0, tq)])
        if masked:
            s = s + bias_ref[sel_ref[e] - 1]
        s_ref[...] = s
        c_ref[...] = jnp.max(s, axis=0, keepdims=True)

    def stage_b(e, s_ref, c_ref, p_ref, a_ref):
        iq = iq_ref[e]
        m_old = m_ref[iq]
        m_new = jnp.maximum(m_old, c_ref[...])
        m_ref[iq] = m_new
        a_ref[...] = jnp.exp2(m_old - m_new)
        p_ref[...] = jnp.exp2(s_ref[...] - m_new).astype(_BF16)

    def stage_c(e, p_ref, a_ref):
        iq = iq_ref[e]
        k0 = pl.multiple_of(jk_ref[e] * tk, tk)
        v1 = jnp.concatenate([vT_ref[:, pl.ds(k0, tk)], jnp.ones((ONES_ROWS, tk), _BF16)], axis=0)
        acc_ref[iq] = a_ref[...] * acc_ref[iq] + _dot(v1, p_ref[...])

    def run(first, n_steps, masked):
        unroll = ATT_UNROLL if n_steps % ATT_UNROLL == 0 else 2
        for _, _, p_ref, a_ref in bufs:
            p_ref[...] = jnp.zeros(p_ref.shape, _BF16)
            a_ref[...] = jnp.ones(a_ref.shape, _F32)
        for d in range(ATT_LEAD):
            stage_a(first + d, *bufs[d % 2][:2], masked)

        def body(u, carry):
            for d in range(unroll):
                e = first + u * unroll + d
                s_ref, c_ref, p_ref, a_ref = bufs[d % 2]
                stage_c(e - ATT_LEAD, p_ref, a_ref)
                stage_b(e, s_ref, c_ref, p_ref, a_ref)
                stage_a(e + ATT_LEAD, s_ref, c_ref, masked)
            return carry

        lax.fori_loop(0, n_steps // unroll, body, 0)
        for d in range(ATT_LEAD):
            stage_c(first + n_steps - ATT_LEAD + d, *bufs[d % 2][2:])

    acc_ref[...] = jnp.zeros(acc_ref.shape, _F32)
    m_ref[...] = jnp.full(m_ref.shape, -jnp.inf, _F32)
    for (first, n_steps), masked in zip(seqs, (False, True)):
        if n_steps:
            run(first, n_steps, masked)
    for iq in range(nq):
        acc = acc_ref[iq]
        o_ref[:, iq * tq:(iq + 1) * tq] = (
            acc[0:V_HEAD_DIM] / acc[V_HEAD_DIM:V_HEAD_DIM + 1]).astype(_BF16)


def _attention(qT, k, vT):
    S = qT.shape[1]
    tq = min(ATT_TQ, S)
    tk = tq // 2
    nq = S // tq
    tables, seqs = _attn_tables(nq)
    assert ATT_LEAD == 2 and ATT_UNROLL % 2 == 0
    grid_spec = pltpu.PrefetchScalarGridSpec(
        num_scalar_prefetch=3,
        grid=(MLA_HEADS,),
        in_specs=[pl.BlockSpec((HEAD_PAD, S), lambda h, *_: (h, 0)),
                  pl.BlockSpec((None, S, HEAD_PAD), lambda h, *_: (h, 0, 0)),
                  pl.BlockSpec((V_HEAD_DIM, S), lambda h, *_: (h, 0))],
        out_specs=pl.BlockSpec((V_HEAD_DIM, S), lambda h, *_: (h, 0)),
        scratch_shapes=[pltpu.VMEM((tk, tq), _F32), pltpu.VMEM((tk, tq), _F32),
                        pltpu.VMEM((1, tq), _F32), pltpu.VMEM((1, tq), _F32),
                        pltpu.VMEM((tk, tq), _BF16), pltpu.VMEM((tk, tq), _BF16),
                        pltpu.VMEM((1, tq), _F32), pltpu.VMEM((1, tq), _F32),
                        pltpu.VMEM((nq, 1, tq), _F32),
                        pltpu.VMEM((nq, V_HEAD_DIM + ONES_ROWS, tq), _F32),
                        pltpu.VMEM((2, tk, tq), _F32)])
    return pl.pallas_call(
        functools.partial(_attn_body, seqs=tuple(seqs)),
        out_shape=jax.ShapeDtypeStruct((MLA_OUT, S), _BF16),
        grid_spec=grid_spec,
        compiler_params=_params(("arbitrary",)),
        name="mla_attention",
    )(*tables, qT, k, vT)


CONV_HALO = 8


def _conv_body(x_ref, g_ref, win_ref, cw_ref, wout_ref, o_ref, cz_ref):
    tm = x_ref.shape[0]
    x = x_ref[...]
    h = _rms_rows(x, g_ref[...]).astype(_BF16)
    W = D_MODEL
    b_gate = _dot(h, win_ref[:, 0:W])
    cz = _dot(h, win_ref[:, W:2 * W]) * _dot(h, win_ref[:, 2 * W:3 * W])

    @pl.when(pl.program_id(0) == 0)
    def _():
        cz_ref[0:CONV_HALO, :] = jnp.zeros((CONV_HALO, W), _F32)

    cz_ref[CONV_HALO:CONV_HALO + tm, :] = cz
    y = cz * cw_ref[2:3, :]
    y = y + cz_ref[CONV_HALO - 1:CONV_HALO - 1 + tm, :] * cw_ref[1:2, :]
    y = y + cz_ref[CONV_HALO - 2:CONV_HALO - 2 + tm, :] * cw_ref[0:1, :]
    cz_ref[0:CONV_HALO, :] = cz[tm - CONV_HALO:tm]
    o_ref[...] = x + _dot((b_gate * y).astype(_BF16), wout_ref[...])


def _conv_mixer(x, g, win, cw, wout):
    S = x.shape[0]
    tm = min(PROJ_TM, S)
    row_spec = pl.BlockSpec((tm, D_MODEL), lambda i: (i, 0))
    return pl.pallas_call(
        _conv_body,
        out_shape=jax.ShapeDtypeStruct((S, D_MODEL), _F32),
        grid=(S // tm,),
        in_specs=[row_spec, _const_spec(g.shape), _const_spec(win.shape),
                  _const_spec(cw.shape), _const_spec(wout.shape)],
        out_specs=row_spec,
        scratch_shapes=[pltpu.VMEM((tm + CONV_HALO, D_MODEL), _F32)],
        compiler_params=_params(("arbitrary",)),
        name="conv_mixer",
    )(x, g, win, cw, wout)


def _prep_even(w_in, w_uq, w_ukv, sg_w):
    winT = w_in.T.astype(_BF16)
    wq = w_uq.reshape(Q_LORA_RANK, MLA_HEADS, QK_HEAD_DIM)
    wq = jnp.pad(wq, ((0, 0), (0, 0), (0, HEAD_PAD - QK_HEAD_DIM)))
    wuqT = wq.reshape(Q_LORA_RANK, MLA_HEADS * HEAD_PAD).T.astype(_BF16)
    wkv = w_ukv.reshape(KV_LORA_RANK, MLA_HEADS, QK_NOPE_DIM + V_HEAD_DIM)
    wuvT = wkv[:, :, QK_NOPE_DIM:].reshape(KV_LORA_RANK, MLA_OUT).T.astype(_BF16)
    wk_nope = jnp.pad(wkv[:, :, :QK_NOPE_DIM], ((0, 0), (0, 0), (0, HEAD_PAD - QK_NOPE_DIM)))
    place = jnp.pad(jnp.eye(QK_ROPE_DIM, dtype=_F32),
                    ((0, 0), (QK_NOPE_DIM, HEAD_PAD - QK_HEAD_DIM)))
    place = jnp.broadcast_to(place[:, None, :], (QK_ROPE_DIM, MLA_HEADS, HEAD_PAD))
    wk = jnp.concatenate([wk_nope, place], axis=0).reshape(
        KV_LORA_RANK + QK_ROPE_DIM, MLA_HEADS * HEAD_PAD).astype(_BF16)
    sgwT = jnp.swapaxes(sg_w, 1, 2)
    return winT, wuqT, wuvT, wk, sgwT


def kernel(x, positions, ffn_pre_norm, ffn_pre_w_gate, ffn_pre_w_up, ffn_pre_w_down, mix_norm, ffn_post_norm, ffn_post_w_gate, ffn_post_w_up, ffn_post_w_down, even_w_in, q_norm, w_uq, kv_norm, w_ukv, sg_norm, sg_w, sg_b, even_w_out, conv_w_in, conv_w, conv_w_out, final_norm):
    B, S, _ = x.shape
    assert B == 1 and S % SG_CHUNK == 0
    xs = x[0]
    inv_freq = ROPE_THETA ** (-jnp.arange(0, QK_ROPE_DIM, 2, dtype=_F32) / QK_ROPE_DIM)
    ang = positions[0].astype(_F32)[None, :] * inv_freq[:, None]
    cosT, sinT = jnp.cos(ang), jnp.sin(ang)
    bf = lambda w: w.astype(_BF16)
    row = lambda v: v.reshape(1, -1)
    col = lambda v: v.reshape(-1, 1)

    for layer in range(DEPTH):
        xs = _ffn(xs, row(ffn_pre_norm[layer]), bf(ffn_pre_w_gate[layer]),
                  bf(ffn_pre_w_up[layer]), bf(ffn_pre_w_down[layer]))
        mix = None
        if layer % 2 == 0:
            e = layer // 2
            winT, wuqT, wuvT, wk, sgwT = _prep_even(even_w_in[e], w_uq[e], w_ukv[e], sg_w[e])
            qT, k, vT, sgT = _even_proj(
                xs, row(mix_norm[layer]), winT, col(q_norm[e]), wuqT, col(kv_norm[e]), wuvT, wk,
                col(sg_norm[e]), sgwT, sg_b[e], cosT, sinT)
            attnT = _attention(qT, k, vT)
            wo = bf(even_w_out[e])
            mix = (attnT, sgT, wo[:MLA_OUT], wo[MLA_OUT:])
        else:
            o = layer // 2
            xs = _conv_mixer(xs, row(mix_norm[layer]), bf(conv_w_in[o]), conv_w[o],
                             bf(conv_w_out[o]))
        xs = _ffn(xs, row(ffn_post_norm[layer]), bf(ffn_post_w_gate[layer]),
                  bf(ffn_post_w_up[layer]), bf(ffn_post_w_down[layer]), mix=mix,
                  final_g=row(final_norm) if layer == DEPTH - 1 else None)
    return xs[None]
```

```python
import functools
import math
from typing import NamedTuple

import jax
import jax.numpy as jnp
from jax import lax
from jax.experimental import pallas as pl
from jax.experimental.pallas import tpu as pltpu

D_MODEL = 1024
DEPTH = 4
MLA_HEADS = 8
QK_NOPE_DIM = 64
QK_ROPE_DIM = 32
V_HEAD_DIM = 64
Q_LORA_RANK = 384
KV_LORA_RANK = 256
ROPE_THETA = 10000.0
SG_GROUPS = 8
SG_GROUP_DIM = 64
SG_WIDTH = SG_GROUPS * SG_GROUP_DIM
SG_CHUNK = 128
CONV_K = 3
D_FF = 2816
NORM_EPS = 1e-6
QK_HEAD_DIM = QK_NOPE_DIM + QK_ROPE_DIM
MLA_OUT = MLA_HEADS * V_HEAD_DIM
EVEN_IN = Q_LORA_RANK + KV_LORA_RANK + QK_ROPE_DIM + 2 * SG_WIDTH
HALF_ROPE = QK_ROPE_DIM // 2

V7X_LANES = 128
V7X_MXU_COLS = 256
V7X_VMEM_BYTES = 64 * 1024 * 1024

HEAD_PAD = V7X_LANES
FFN_TM = 512
FFN_TF = V7X_MXU_COLS
FFN_CAST_BANDS = 16
PROJ_TM = 512
ATT_TQ = 1024
VMEM_LIMIT = 56 * 1024 * 1024

_NT = (((1,), (1,)), ((), ()))
_TN = (((0,), (0,)), ((), ()))
_F32 = jnp.float32
_BF16 = jnp.bfloat16


def _dot(a, b, dims=None):
    if dims is None:
        return jnp.dot(a, b, preferred_element_type=_F32)
    return lax.dot_general(a, b, dims, preferred_element_type=_F32)


def _rms_rows(x, g):
    ms = jnp.mean(x * x, axis=-1, keepdims=True)
    return x * lax.rsqrt(ms + NORM_EPS) * g


def _rms_cols(xT, g):
    ms = jnp.mean(xT * xT, axis=0, keepdims=True)
    return xT * lax.rsqrt(ms + NORM_EPS) * g


class _Slab(NamedTuple):
    stack: jax.Array
    index: int
    part: int = 0
    parts: int = 1


def _operand(p):
    return p.stack if isinstance(p, _Slab) else p


def _const_spec(p):
    if isinstance(p, _Slab):
        _, rows, cols = p.stack.shape
        return pl.BlockSpec((None, rows // p.parts, cols), lambda *_: (p.index, p.part, 0),
                            pipeline_mode=pl.Buffered(1))
    n = p.ndim
    return pl.BlockSpec(p.shape, lambda *_: (0,) * n, pipeline_mode=pl.Buffered(1))


def _params(semantics, flags=None):
    return pltpu.CompilerParams(dimension_semantics=semantics,
                                vmem_limit_bytes=VMEM_LIMIT, flags=flags)


def _ffn_body(has_mix, has_final, has_next, *refs):
    it = iter(refs)
    x_ref = next(it)
    if has_mix:
        attn_ref, sg_ref, woa_ref, wos_ref = next(it), next(it), next(it), next(it)
    g_ref, wg_ref, wu_ref, wd_ref = next(it), next(it), next(it), next(it)
    gf_ref = next(it) if has_final else None
    next_f32 = [next(it) for _ in range(3)] if has_next else []
    o_ref = next(it)
    next_bf16 = [next(it) for _ in range(3)] if has_next else []
    h_ref, a_ref = next(it), next(it)

    for src_ref, dst_ref in zip(next_f32, next_bf16):
        dst_ref[...] = src_ref[...].astype(_BF16)

    x = x_ref[...]
    if has_mix:
        x = x + _dot(attn_ref[...], woa_ref[...], _TN) + _dot(sg_ref[...], wos_ref[...], _TN)
    h_ref[...] = _rms_rows(x, g_ref[...]).astype(_BF16)
    for j in range(D_FF // FFN_TF):
        cols = slice(j * FFN_TF, (j + 1) * FFN_TF)
        gate = _dot(h_ref[...], wg_ref[:, cols])
        up = _dot(h_ref[...], wu_ref[:, cols])
        a_ref[:, cols] = (gate * jax.nn.sigmoid(gate) * up).astype(_BF16)
    out = x + 0.5 * _dot(a_ref[...], wd_ref[...])
    if has_final:
        out = _rms_rows(out, gf_ref[...])
    o_ref[...] = out


def _ffn(x, g, wg, wu, wd, mix=None, final_g=None, nxt=None):
    S = x.shape[0]
    tm = min(FFN_TM, S)
    steps = S // tm
    row_spec = pl.BlockSpec((tm, D_MODEL), lambda i: (i, 0))
    args, specs = [x], [row_spec]
    out_shape, out_specs = [jax.ShapeDtypeStruct((S, D_MODEL), _F32)], [row_spec]
    if mix is not None:
        attnT, sgT, woa, wos = mix
        colT = pl.BlockSpec((MLA_OUT, tm), lambda i: (0, i))
        args += [attnT, sgT, woa, wos]
        specs += [colT, colT, _const_spec(woa), _const_spec(wos)]
    args += [g, wg, wu, wd]
    specs += [_const_spec(g), _const_spec(wg), _const_spec(wu), _const_spec(wd)]
    if final_g is not None:
        args.append(final_g)
        specs.append(_const_spec(final_g))
    if nxt is not None:
        layer, *stacks = nxt
        bands = math.gcd(steps, FFN_CAST_BANDS)
        for w in stacks:
            _, rows, cols = w.shape
            args.append(w)
            specs.append(pl.BlockSpec((None, rows // bands, cols),
                                      lambda i: (layer, i * bands // steps, 0)))
            out_shape.append(jax.ShapeDtypeStruct((rows, cols), _BF16))
            out_specs.append(pl.BlockSpec((rows // bands, cols),
                                          lambda i: (i * bands // steps, 0)))
    outs = pl.pallas_call(
        functools.partial(_ffn_body, mix is not None, final_g is not None, nxt is not None),
        out_shape=tuple(out_shape),
        grid=(steps,),
        in_specs=specs,
        out_specs=tuple(out_specs),
        scratch_shapes=[pltpu.VMEM((tm, D_MODEL), _BF16), pltpu.VMEM((tm, D_FF), _BF16)],
        compiler_params=_params(("arbitrary",)),
        name="ffn_mix" if mix is not None else ("ffn_final" if final_g is not None else "ffn"),
    )(*map(_operand, args))
    return outs[0], tuple(outs[1:])


def _even_proj_body(x_ref, g_ref, winT_ref, qn_ref, wuqT_ref, kvn_ref, wuvT_ref,
                    wk_ref, sgn_ref, sgwT_ref, sgb_ref, cos_ref, sin_ref,
                    qT_ref, k_ref, vT_ref, sgT_ref):
    tm = x_ref.shape[0]
    h = _rms_rows(x_ref[...], g_ref[...]).astype(_BF16)
    projT = _dot(winT_ref[...], h, _NT)
    o_kv = Q_LORA_RANK
    o_kr = o_kv + KV_LORA_RANK
    o_z = o_kr + QK_ROPE_DIM
    cos = cos_ref[...]
    sin = sin_ref[...]

    cqn = _rms_cols(projT[0:o_kv], qn_ref[...]).astype(_BF16)
    qT = _dot(wuqT_ref[...], cqn) * (QK_HEAD_DIM ** -0.5 * math.log2(math.e))
    pieces = []
    for hd in range(MLA_HEADS):
        b = hd * HEAD_PAD
        t1 = qT[b + QK_NOPE_DIM:b + QK_NOPE_DIM + HALF_ROPE]
        t2 = qT[b + QK_NOPE_DIM + HALF_ROPE:b + QK_HEAD_DIM]
        pieces += [qT[b:b + QK_NOPE_DIM], t1 * cos - t2 * sin, t1 * sin + t2 * cos,
                   qT[b + QK_HEAD_DIM:b + HEAD_PAD]]
    qT_ref[...] = jnp.concatenate(pieces, axis=0).astype(_BF16)

    ckvn = _rms_cols(projT[o_kv:o_kr], kvn_ref[...]).astype(_BF16)
    vT_ref[...] = _dot(wuvT_ref[...], ckvn).astype(_BF16)
    k1 = projT[o_kr:o_kr + HALF_ROPE]
    k2 = projT[o_kr + HALF_ROPE:o_z]
    kro = jnp.concatenate([k1 * cos - k2 * sin, k1 * sin + k2 * cos], axis=0)
    lhsT = jnp.concatenate([ckvn, kro.astype(_BF16)], axis=0)
    k_all = _dot(lhsT, wk_ref[...], _TN).astype(_BF16)
    for hd in range(MLA_HEADS):
        k_ref[hd] = k_all[:, hd * HEAD_PAD:(hd + 1) * HEAD_PAD]

    z = projT[o_z:EVEN_IN]
    z = 0.5 * z * (1.0 + lax.erf(z * (2.0 ** -0.5)))
    uT = z[0:SG_WIDTH]
    vn = _rms_cols(z[SG_WIDTH:2 * SG_WIDTH], sgn_ref[...]).astype(_BF16)
    nchunk = tm // SG_CHUNK
    s_idx = lax.broadcasted_iota(jnp.int32, (SG_CHUNK, SG_CHUNK), 0)
    t_idx = lax.broadcasted_iota(jnp.int32, (SG_CHUNK, SG_CHUNK), 1)
    causal = s_idx <= t_idx
    rows = []
    for gi in range(SG_GROUPS):
        r0 = gi * SG_GROUP_DIM
        lhs = jnp.concatenate(
            [vn[r0:r0 + SG_GROUP_DIM, n * SG_CHUNK:(n + 1) * SG_CHUNK] for n in range(nchunk)],
            axis=0)
        wT = jnp.where(causal, sgwT_ref[gi * SG_CHUNK:(gi + 1) * SG_CHUNK, :], 0.0)
        wT = wT.astype(_BF16)
        mixed = _dot(lhs, wT) + sgb_ref[gi:gi + 1, :]
        rows.append(jnp.concatenate(
            [mixed[n * SG_GROUP_DIM:(n + 1) * SG_GROUP_DIM] for n in range(nchunk)], axis=1))
    sgT_ref[...] = (uT * jnp.concatenate(rows, axis=0)).astype(_BF16)


def _even_proj(x, g, winT, qn, wuqT, kvn, wuvT, wk, sgn, sgwT, sgb, cosT, sinT):
    S = x.shape[0]
    tm = min(PROJ_TM, S)
    colT = lambda rows: pl.BlockSpec((rows, tm), lambda i: (0, i))
    consts = [g, winT, qn, wuqT, kvn, wuvT, wk, sgn, sgwT, sgb]
    return pl.pallas_call(
        _even_proj_body,
        out_shape=(jax.ShapeDtypeStruct((MLA_HEADS * HEAD_PAD, S), _BF16),
                   jax.ShapeDtypeStruct((MLA_HEADS, S, HEAD_PAD), _BF16),
                   jax.ShapeDtypeStruct((MLA_OUT, S), _BF16),
                   jax.ShapeDtypeStruct((SG_WIDTH, S), _BF16)),
        grid=(S // tm,),
        in_specs=[pl.BlockSpec((tm, D_MODEL), lambda i: (i, 0))]
        + [_const_spec(c) for c in consts]
        + [colT(HALF_ROPE), colT(HALF_ROPE)],
        out_specs=(colT(MLA_HEADS * HEAD_PAD),
                   pl.BlockSpec((MLA_HEADS, tm, HEAD_PAD), lambda i: (0, i, 0)),
                   colT(MLA_OUT), colT(SG_WIDTH)),
        compiler_params=_params(("arbitrary",)),
        name="even_proj",
    )(x, *map(_operand, consts), cosT, sinT)


ONES_ROWS = 16


ATT_LEAD = 2
ATT_UNROLL = 8


def _attn_tables(nq):
    below = [(iq, jk, 0) for iq in range(nq) for jk in range(2 * iq)]
    diag = [(iq, 2 * iq + d, 1 + d) for iq in range(nq) for d in range(2)]
    rows, seqs = [], []
    for seq in (below, diag):
        if seq:
            seqs.append((len(rows) + ATT_LEAD, len(seq)))
            rows += [(0, 0, 0)] * ATT_LEAD + seq + [seq[-1]] * ATT_LEAD
        else:
            seqs.append((0, 0))
    return tuple(jnp.asarray([r[c] for r in rows], jnp.int32) for c in range(3)), seqs


def _attn_body(iq_ref, jk_ref, sel_ref, qT_ref, k_ref, vT_ref, o_ref,
               s0_ref, s1_ref, c0_ref, c1_ref, p0_ref, p1_ref, a0_ref, a1_ref,
               m_ref, acc_ref, bias_ref, *, seqs):
    tk, tq = s0_ref.shape
    nq = acc_ref.shape[0]
    bufs = ((s0_ref, c0_ref, p0_ref, a0_ref), (s1_ref, c1_ref, p1_ref, a1_ref))

    @pl.when(pl.program_id(0) == 0)
    def _():
        r = lax.broadcasted_iota(jnp.int32, (tk, tq), 0)
        q = lax.broadcasted_iota(jnp.int32, (tk, tq), 1)
        bias_ref[0] = jnp.where(r <= q, 0.0, -jnp.inf)
        bias_ref[1] = jnp.where(r + tk <= q, 0.0, -jnp.inf)

    def stage_a(e, s_ref, c_ref, masked):
        k0 = pl.multiple_of(jk_ref[e] * tk, tk)
        q0 = pl.multiple_of(iq_ref[e] * tq, tq)
        s = _dot(k_ref[pl.ds(k0, tk), :], qT_ref[:, pl.ds(q0, tq)])
        if masked:
            s = s + bias_ref[sel_ref[e] - 1]
        s_ref[...] = s
        c_ref[...] = jnp.max(s, axis=0, keepdims=True)

    def stage_b(e, s_ref, c_ref, p_ref, a_ref):
        iq = iq_ref[e]
        m_old = m_ref[iq]
        m_new = jnp.maximum(m_old, c_ref[...])
        m_ref[iq] = m_new
        a_ref[...] = jnp.exp2(m_old - m_new)
        p_ref[...] = jnp.exp2(s_ref[...] - m_new).astype(_BF16)

    def stage_c(e, p_ref, a_ref):
        iq = iq_ref[e]
        k0 = pl.multiple_of(jk_ref[e] * tk, tk)
        v1 = jnp.concatenate([vT_ref[:, pl.ds(k0, tk)], jnp.ones((ONES_ROWS, tk), _BF16)], axis=0)
        acc_ref[iq] = a_ref[...] * acc_ref[iq] + _dot(v1, p_ref[...])

    def run(first, n_steps, masked):
        unroll = ATT_UNROLL if n_steps % ATT_UNROLL == 0 else 2
        for _, _, p_ref, a_ref in bufs:
            p_ref[...] = jnp.zeros(p_ref.shape, _BF16)
            a_ref[...] = jnp.ones(a_ref.shape, _F32)
        for d in range(ATT_LEAD):
            stage_a(first + d, *bufs[d % 2][:2], masked)

        def body(u, carry):
            for d in range(unroll):
                e = first + u * unroll + d
                s_ref, c_ref, p_ref, a_ref = bufs[d % 2]
                stage_c(e - ATT_LEAD, p_ref, a_ref)
                stage_b(e, s_ref, c_ref, p_ref, a_ref)
                stage_a(e + ATT_LEAD, s_ref, c_ref, masked)
            return carry

        lax.fori_loop(0, n_steps // unroll, body, 0)
        for d in range(ATT_LEAD):
            stage_c(first + n_steps - ATT_LEAD + d, *bufs[d % 2][2:])

    acc_ref[...] = jnp.zeros(acc_ref.shape, _F32)
    m_ref[...] = jnp.full(m_ref.shape, -jnp.inf, _F32)
    for (first, n_steps), masked in zip(seqs, (False, True)):
        if n_steps:
            run(first, n_steps, masked)
    for iq in range(nq):
        acc = acc_ref[iq]
        o_ref[:, iq * tq:(iq + 1) * tq] = (
            acc[0:V_HEAD_DIM] / acc[V_HEAD_DIM:V_HEAD_DIM + 1]).astype(_BF16)


def _attention(qT, k, vT):
    S = qT.shape[1]
    tq = min(ATT_TQ, S)
    tk = tq // 2
    nq = S // tq
    tables, seqs = _attn_tables(nq)
    assert ATT_LEAD == 2 and ATT_UNROLL % 2 == 0
    grid_spec = pltpu.PrefetchScalarGridSpec(
        num_scalar_prefetch=3,
        grid=(MLA_HEADS,),
        in_specs=[pl.BlockSpec((HEAD_PAD, S), lambda h, *_: (h, 0)),
                  pl.BlockSpec((None, S, HEAD_PAD), lambda h, *_: (h, 0, 0)),
                  pl.BlockSpec((V_HEAD_DIM, S), lambda h, *_: (h, 0))],
        out_specs=pl.BlockSpec((V_HEAD_DIM, S), lambda h, *_: (h, 0)),
        scratch_shapes=[pltpu.VMEM((tk, tq), _F32), pltpu.VMEM((tk, tq), _F32),
                        pltpu.VMEM((1, tq), _F32), pltpu.VMEM((1, tq), _F32),
                        pltpu.VMEM((tk, tq), _BF16), pltpu.VMEM((tk, tq), _BF16),
                        pltpu.VMEM((1, tq), _F32), pltpu.VMEM((1, tq), _F32),
                        pltpu.VMEM((nq, 1, tq), _F32),
                        pltpu.VMEM((nq, V_HEAD_DIM + ONES_ROWS, tq), _F32),
                        pltpu.VMEM((2, tk, tq), _F32)])
    return pl.pallas_call(
        functools.partial(_attn_body, seqs=tuple(seqs)),
        out_shape=jax.ShapeDtypeStruct((MLA_OUT, S), _BF16),
        grid_spec=grid_spec,
        compiler_params=_params(("arbitrary",)),
        name="mla_attention",
    )(*tables, qT, k, vT)


CONV_HALO = 8


def _conv_body(x_ref, g_ref, win_ref, cw_ref, wout_ref, o_ref, cz_ref):
    tm = x_ref.shape[0]
    x = x_ref[...]
    h = _rms_rows(x, g_ref[...]).astype(_BF16)
    W = D_MODEL
    b_gate = _dot(h, win_ref[:, 0:W])
    cz = _dot(h, win_ref[:, W:2 * W]) * _dot(h, win_ref[:, 2 * W:3 * W])

    @pl.when(pl.program_id(0) == 0)
    def _():
        cz_ref[0:CONV_HALO, :] = jnp.zeros((CONV_HALO, W), _F32)

    cz_ref[CONV_HALO:CONV_HALO + tm, :] = cz
    y = cz * cw_ref[2:3, :]
    y = y + cz_ref[CONV_HALO - 1:CONV_HALO - 1 + tm, :] * cw_ref[1:2, :]
    y = y + cz_ref[CONV_HALO - 2:CONV_HALO - 2 + tm, :] * cw_ref[0:1, :]
    cz_ref[0:CONV_HALO, :] = cz[tm - CONV_HALO:tm]
    o_ref[...] = x + _dot((b_gate * y).astype(_BF16), wout_ref[...])


def _conv_mixer(x, g, win, cw, wout):
    S = x.shape[0]
    tm = min(PROJ_TM, S)
    row_spec = pl.BlockSpec((tm, D_MODEL), lambda i: (i, 0))
    return pl.pallas_call(
        _conv_body,
        out_shape=jax.ShapeDtypeStruct((S, D_MODEL), _F32),
        grid=(S // tm,),
        in_specs=[row_spec, _const_spec(g), _const_spec(win), _const_spec(cw),
                  _const_spec(wout)],
        out_specs=row_spec,
        scratch_shapes=[pltpu.VMEM((tm + CONV_HALO, D_MODEL), _F32)],
        compiler_params=_params(("arbitrary",)),
        name="conv_mixer",
    )(x, *map(_operand, (g, win, cw, wout)))


def _prep_even(w_in, w_uq, w_ukv, sg_w):
    n = w_in.shape[0]
    winT = jnp.swapaxes(w_in, 1, 2).astype(_BF16)
    wq = w_uq.reshape(n, Q_LORA_RANK, MLA_HEADS, QK_HEAD_DIM)
    wq = jnp.pad(wq, ((0, 0), (0, 0), (0, 0), (0, HEAD_PAD - QK_HEAD_DIM)))
    wuqT = jnp.swapaxes(wq.reshape(n, Q_LORA_RANK, MLA_HEADS * HEAD_PAD), 1, 2).astype(_BF16)
    wkv = w_ukv.reshape(n, KV_LORA_RANK, MLA_HEADS, QK_NOPE_DIM + V_HEAD_DIM)
    wuvT = jnp.swapaxes(wkv[..., QK_NOPE_DIM:].reshape(n, KV_LORA_RANK, MLA_OUT), 1, 2)
    wk_nope = jnp.pad(wkv[..., :QK_NOPE_DIM],
                      ((0, 0), (0, 0), (0, 0), (0, HEAD_PAD - QK_NOPE_DIM)))
    place = jnp.pad(jnp.eye(QK_ROPE_DIM, dtype=_F32),
                    ((0, 0), (QK_NOPE_DIM, HEAD_PAD - QK_HEAD_DIM)))
    place = jnp.broadcast_to(place[None, :, None, :], (n, QK_ROPE_DIM, MLA_HEADS, HEAD_PAD))
    wk = jnp.concatenate([wk_nope, place], axis=1).reshape(
        n, KV_LORA_RANK + QK_ROPE_DIM, MLA_HEADS * HEAD_PAD).astype(_BF16)
    sgwT = jnp.swapaxes(sg_w, 2, 3).reshape(n, SG_GROUPS * SG_CHUNK, SG_CHUNK)
    return winT, wuqT, wuvT.astype(_BF16), wk, sgwT


def kernel(x, positions, ffn_pre_norm, ffn_pre_w_gate, ffn_pre_w_up, ffn_pre_w_down, mix_norm, ffn_post_norm, ffn_post_w_gate, ffn_post_w_up, ffn_post_w_down, even_w_in, q_norm, w_uq, kv_norm, w_ukv, sg_norm, sg_w, sg_b, even_w_out, conv_w_in, conv_w, conv_w_out, final_norm):
    B, S, _ = x.shape
    assert B == 1 and S % SG_CHUNK == 0
    xs = x[0]
    inv_freq = ROPE_THETA ** (-jnp.arange(0, QK_ROPE_DIM, 2, dtype=_F32) / QK_ROPE_DIM)
    ang = positions[0].astype(_F32)[None, :] * inv_freq[:, None]
    cosT, sinT = jnp.cos(ang), jnp.sin(ang)
    bf = lambda w: w.astype(_BF16)
    rows = lambda v: v[:, None, :]
    cols = lambda v: v[:, :, None]

    pre_w = (ffn_pre_w_gate, ffn_pre_w_up, ffn_pre_w_down)
    post_w = (ffn_post_w_gate, ffn_post_w_up, ffn_post_w_down)
    pre_g, post_g, mix_g = rows(ffn_pre_norm), rows(ffn_post_norm), rows(mix_norm)
    w_bf16 = tuple(bf(w[0]) for w in pre_w)
    winT, wuqT, wuvT, wk, sgwT = _prep_even(even_w_in, w_uq, w_ukv, sg_w)
    qn, kvn, sgn, wo = cols(q_norm), cols(kv_norm), cols(sg_norm), bf(even_w_out)
    conv = (bf(conv_w_in), conv_w, bf(conv_w_out))

    for layer in range(DEPTH):
        xs, w_bf16 = _ffn(xs, _Slab(pre_g, layer), *w_bf16, nxt=(layer, *post_w))
        mix = None
        if layer % 2 == 0:
            e = layer // 2
            sl = lambda w: _Slab(w, e)
            qT, k, vT, sgT = _even_proj(
                xs, _Slab(mix_g, layer), sl(winT), sl(qn), sl(wuqT), sl(kvn), sl(wuvT), sl(wk),
                sl(sgn), sl(sgwT), sl(sg_b), cosT, sinT)
            attnT = _attention(qT, k, vT)
            mix = (attnT, sgT, _Slab(wo, e, 0, 2), _Slab(wo, e, 1, 2))
        else:
            o = layer // 2
            xs = _conv_mixer(xs, _Slab(mix_g, layer), *(_Slab(w, o) for w in conv))
        last = layer == DEPTH - 1
        xs, w_bf16 = _ffn(xs, _Slab(post_g, layer), *w_bf16, mix=mix,
                          final_g=final_norm.reshape(1, -1) if last else None,
                          nxt=None if last else (layer + 1, *pre_w))
    return xs[None]
```

```python
import functools
import math
from typing import NamedTuple

import jax
import jax.numpy as jnp
from jax import lax
from jax.experimental import pallas as pl
from jax.experimental.pallas import tpu as pltpu

D_MODEL = 1024
DEPTH = 4
MLA_HEADS = 8
QK_NOPE_DIM = 64
QK_ROPE_DIM = 32
V_HEAD_DIM = 64
Q_LORA_RANK = 384
KV_LORA_RANK = 256
ROPE_THETA = 10000.0
SG_GROUPS = 8
SG_GROUP_DIM = 64
SG_WIDTH = SG_GROUPS * SG_GROUP_DIM
SG_CHUNK = 128
CONV_K = 3
D_FF = 2816
NORM_EPS = 1e-6
QK_HEAD_DIM = QK_NOPE_DIM + QK_ROPE_DIM
MLA_OUT = MLA_HEADS * V_HEAD_DIM
EVEN_IN = Q_LORA_RANK + KV_LORA_RANK + QK_ROPE_DIM + 2 * SG_WIDTH
HALF_ROPE = QK_ROPE_DIM // 2

V7X_LANES = 128
V7X_MXU_COLS = 256
V7X_VMEM_BYTES = 64 * 1024 * 1024

HEAD_PAD = V7X_LANES
FFN_TM = 1024
FFN_TF = V7X_MXU_COLS
FFN_CAST_BANDS = 16
PROJ_TM = 1024
ATT_TQ = 1024
VMEM_LIMIT = 56 * 1024 * 1024

_NT = (((1,), (1,)), ((), ()))
_TN = (((0,), (0,)), ((), ()))
_F32 = jnp.float32
_BF16 = jnp.bfloat16


def _dot(a, b, dims=None):
    if dims is None:
        return jnp.dot(a, b, preferred_element_type=_F32)
    return lax.dot_general(a, b, dims, preferred_element_type=_F32)


def _rms_rows(x, g):
    ms = jnp.mean(x * x, axis=-1, keepdims=True)
    return x * lax.rsqrt(ms + NORM_EPS) * g


def _rms_cols(xT, g):
    ms = jnp.mean(xT * xT, axis=0, keepdims=True)
    return xT * lax.rsqrt(ms + NORM_EPS) * g


class _Slab(NamedTuple):
    stack: jax.Array
    index: int
    part: int = 0
    parts: int = 1


def _operand(p):
    return p.stack if isinstance(p, _Slab) else p


def _const_spec(p):
    if isinstance(p, _Slab):
        _, rows, cols = p.stack.shape
        return pl.BlockSpec((None, rows // p.parts, cols), lambda *_: (p.index, p.part, 0),
                            pipeline_mode=pl.Buffered(1))
    n = p.ndim
    return pl.BlockSpec(p.shape, lambda *_: (0,) * n, pipeline_mode=pl.Buffered(1))


def _params(semantics, flags=None):
    return pltpu.CompilerParams(dimension_semantics=semantics,
                                vmem_limit_bytes=VMEM_LIMIT, flags=flags)


def _ffn_body(has_mix, has_final, has_next, *refs):
    it = iter(refs)
    x_ref = next(it)
    if has_mix:
        attn_ref, sg_ref, woa_ref, wos_ref = next(it), next(it), next(it), next(it)
    g_ref, wg_ref, wu_ref, wd_ref = next(it), next(it), next(it), next(it)
    gf_ref = next(it) if has_final else None
    next_f32 = [next(it) for _ in range(3)] if has_next else []
    o_ref = next(it)
    next_bf16 = [next(it) for _ in range(3)] if has_next else []
    h_ref, a_ref = next(it), next(it)

    for src_ref, dst_ref in zip(next_f32, next_bf16):
        dst_ref[...] = src_ref[...].astype(_BF16)

    x = x_ref[...]
    if has_mix:
        x = x + _dot(attn_ref[...], woa_ref[...], _TN) + _dot(sg_ref[...], wos_ref[...], _TN)
    h_ref[...] = _rms_rows(x, g_ref[...]).astype(_BF16)
    for j in range(D_FF // FFN_TF):
        cols = slice(j * FFN_TF, (j + 1) * FFN_TF)
        gate = _dot(h_ref[...], wg_ref[:, cols])
        up = _dot(h_ref[...], wu_ref[:, cols])
        a_ref[:, cols] = (gate * jax.nn.sigmoid(gate) * up).astype(_BF16)
    out = x + 0.5 * _dot(a_ref[...], wd_ref[...])
    if has_final:
        out = _rms_rows(out, gf_ref[...])
    o_ref[...] = out


def _ffn(x, g, wg, wu, wd, mix=None, final_g=None, nxt=None):
    S = x.shape[0]
    tm = min(FFN_TM, S)
    steps = S // tm
    row_spec = pl.BlockSpec((tm, D_MODEL), lambda i: (i, 0))
    args, specs = [x], [row_spec]
    out_shape, out_specs = [jax.ShapeDtypeStruct((S, D_MODEL), _F32)], [row_spec]
    if mix is not None:
        attnT, sgT, woa, wos = mix
        colT = pl.BlockSpec((MLA_OUT, tm), lambda i: (0, i))
        args += [attnT, sgT, woa, wos]
        specs += [colT, colT, _const_spec(woa), _const_spec(wos)]
    args += [g, wg, wu, wd]
    specs += [_const_spec(g), _const_spec(wg), _const_spec(wu), _const_spec(wd)]
    if final_g is not None:
        args.append(final_g)
        specs.append(_const_spec(final_g))
    if nxt is not None:
        layer, *stacks = nxt
        bands = math.gcd(steps, FFN_CAST_BANDS)
        for w in stacks:
            _, rows, cols = w.shape
            args.append(w)
            specs.append(pl.BlockSpec((None, rows // bands, cols),
                                      lambda i: (layer, i * bands // steps, 0)))
            out_shape.append(jax.ShapeDtypeStruct((rows, cols), _BF16))
            out_specs.append(pl.BlockSpec((rows // bands, cols),
                                          lambda i: (i * bands // steps, 0)))
    outs = pl.pallas_call(
        functools.partial(_ffn_body, mix is not None, final_g is not None, nxt is not None),
        out_shape=tuple(out_shape),
        grid=(steps,),
        in_specs=specs,
        out_specs=tuple(out_specs),
        scratch_shapes=[pltpu.VMEM((tm, D_MODEL), _BF16), pltpu.VMEM((tm, D_FF), _BF16)],
        compiler_params=_params(("arbitrary",)),
        name="ffn_mix" if mix is not None else ("ffn_final" if final_g is not None else "ffn"),
    )(*map(_operand, args))
    return outs[0], tuple(outs[1:])


def _even_proj_body(x_ref, g_ref, winT_ref, qn_ref, wuqT_ref, kvn_ref, wuvT_ref,
                    wk_ref, sgn_ref, sgwT_ref, sgb_ref, cos_ref, sin_ref,
                    qT_ref, k_ref, vT_ref, sgT_ref):
    tm = x_ref.shape[0]
    h = _rms_rows(x_ref[...], g_ref[...]).astype(_BF16)
    projT = _dot(winT_ref[...], h, _NT)
    o_kv = Q_LORA_RANK
    o_kr = o_kv + KV_LORA_RANK
    o_z = o_kr + QK_ROPE_DIM
    cos = cos_ref[...]
    sin = sin_ref[...]

    cqn = _rms_cols(projT[0:o_kv], qn_ref[...]).astype(_BF16)
    qT = _dot(wuqT_ref[...], cqn) * (QK_HEAD_DIM ** -0.5 * math.log2(math.e))
    pieces = []
    for hd in range(MLA_HEADS):
        b = hd * HEAD_PAD
        t1 = qT[b + QK_NOPE_DIM:b + QK_NOPE_DIM + HALF_ROPE]
        t2 = qT[b + QK_NOPE_DIM + HALF_ROPE:b + QK_HEAD_DIM]
        pieces += [qT[b:b + QK_NOPE_DIM], t1 * cos - t2 * sin, t1 * sin + t2 * cos,
                   qT[b + QK_HEAD_DIM:b + HEAD_PAD]]
    qT_ref[...] = jnp.concatenate(pieces, axis=0).astype(_BF16)

    ckvn = _rms_cols(projT[o_kv:o_kr], kvn_ref[...]).astype(_BF16)
    vT_ref[...] = _dot(wuvT_ref[...], ckvn).astype(_BF16)
    k1 = projT[o_kr:o_kr + HALF_ROPE]
    k2 = projT[o_kr + HALF_ROPE:o_z]
    kro = jnp.concatenate([k1 * cos - k2 * sin, k1 * sin + k2 * cos], axis=0)
    lhsT = jnp.concatenate([ckvn, kro.astype(_BF16)], axis=0)
    k_all = _dot(lhsT, wk_ref[...], _TN).astype(_BF16)
    for hd in range(MLA_HEADS):
        k_ref[hd] = k_all[:, hd * HEAD_PAD:(hd + 1) * HEAD_PAD]

    z = projT[o_z:EVEN_IN]
    z = 0.5 * z * (1.0 + lax.erf(z * (2.0 ** -0.5)))
    uT = z[0:SG_WIDTH]
    vn = _rms_cols(z[SG_WIDTH:2 * SG_WIDTH], sgn_ref[...]).astype(_BF16)
    nchunk = tm // SG_CHUNK
    s_idx = lax.broadcasted_iota(jnp.int32, (SG_CHUNK, SG_CHUNK), 0)
    t_idx = lax.broadcasted_iota(jnp.int32, (SG_CHUNK, SG_CHUNK), 1)
    causal = s_idx <= t_idx
    rows = []
    for gi in range(SG_GROUPS):
        r0 = gi * SG_GROUP_DIM
        lhs = jnp.concatenate(
            [vn[r0:r0 + SG_GROUP_DIM, n * SG_CHUNK:(n + 1) * SG_CHUNK] for n in range(nchunk)],
            axis=0)
        wT = jnp.where(causal, sgwT_ref[gi * SG_CHUNK:(gi + 1) * SG_CHUNK, :], 0.0)
        wT = wT.astype(_BF16)
        mixed = _dot(lhs, wT) + sgb_ref[gi:gi + 1, :]
        rows.append(jnp.concatenate(
            [mixed[n * SG_GROUP_DIM:(n + 1) * SG_GROUP_DIM] for n in range(nchunk)], axis=1))
    sgT_ref[...] = (uT * jnp.concatenate(rows, axis=0)).astype(_BF16)


def _even_proj(x, g, winT, qn, wuqT, kvn, wuvT, wk, sgn, sgwT, sgb, cosT, sinT):
    S = x.shape[0]
    tm = min(PROJ_TM, S)
    colT = lambda rows: pl.BlockSpec((rows, tm), lambda i: (0, i))
    consts = [g, winT, qn, wuqT, kvn, wuvT, wk, sgn, sgwT, sgb]
    return pl.pallas_call(
        _even_proj_body,
        out_shape=(jax.ShapeDtypeStruct((MLA_HEADS * HEAD_PAD, S), _BF16),
                   jax.ShapeDtypeStruct((MLA_HEADS, S, HEAD_PAD), _BF16),
                   jax.ShapeDtypeStruct((MLA_OUT, S), _BF16),
                   jax.ShapeDtypeStruct((SG_WIDTH, S), _BF16)),
        grid=(S // tm,),
        in_specs=[pl.BlockSpec((tm, D_MODEL), lambda i: (i, 0))]
        + [_const_spec(c) for c in consts]
        + [colT(HALF_ROPE), colT(HALF_ROPE)],
        out_specs=(colT(MLA_HEADS * HEAD_PAD),
                   pl.BlockSpec((MLA_HEADS, tm, HEAD_PAD), lambda i: (0, i, 0)),
                   colT(MLA_OUT), colT(SG_WIDTH)),
        compiler_params=_params(("arbitrary",)),
        name="even_proj",
    )(x, *map(_operand, consts), cosT, sinT)


ONES_ROWS = 16


ATT_LEAD = 2
ATT_UNROLL = 8


def _attn_tables(nq):
    below = [(iq, jk, 0) for iq in range(nq) for jk in range(2 * iq)]
    diag = [(iq, 2 * iq + d, 1 + d) for iq in range(nq) for d in range(2)]
    rows, seqs = [], []
    for seq in (below, diag):
        if seq:
            seqs.append((len(rows) + ATT_LEAD, len(seq)))
            rows += [(0, 0, 0)] * ATT_LEAD + seq + [seq[-1]] * ATT_LEAD
        else:
            seqs.append((0, 0))
    return tuple(jnp.asarray([r[c] for r in rows], jnp.int32) for c in range(3)), seqs


def _attn_body(iq_ref, jk_ref, sel_ref, qT_ref, k_ref, vT_ref, o_ref,
               s0_ref, s1_ref, c0_ref, c1_ref, p0_ref, p1_ref, a0_ref, a1_ref,
               m_ref, acc_ref, bias_ref, *, seqs):
    tk, tq = s0_ref.shape
    nq = acc_ref.shape[0]
    bufs = ((s0_ref, c0_ref, p0_ref, a0_ref), (s1_ref, c1_ref, p1_ref, a1_ref))

    @pl.when(pl.program_id(0) == 0)
    def _():
        r = lax.broadcasted_iota(jnp.int32, (tk, tq), 0)
        q = lax.broadcasted_iota(jnp.int32, (tk, tq), 1)
        bias_ref[0] = jnp.where(r <= q, 0.0, -jnp.inf)
        bias_ref[1] = jnp.where(r + tk <= q, 0.0, -jnp.inf)

    def stage_a(e, s_ref, c_ref, masked):
        k0 = pl.multiple_of(jk_ref[e] * tk, tk)
        q0 = pl.multiple_of(iq_ref[e] * tq, tq)
        s = _dot(k_ref[pl.ds(k0, tk), :], qT_ref[:, pl.ds(q0, tq)])
        if masked:
            s = s + bias_ref[sel_ref[e] - 1]
        s_ref[...] = s
        c_ref[...] = jnp.max(s, axis=0, keepdims=True)

    def stage_b(e, s_ref, c_ref, p_ref, a_ref):
        iq = iq_ref[e]
        m_old = m_ref[iq]
        m_new = jnp.maximum(m_old, c_ref[...])
        m_ref[iq] = m_new
        a_ref[...] = jnp.exp2(m_old - m_new)
        p_ref[...] = jnp.exp2(s_ref[...] - m_new).astype(_BF16)

    def stage_c(e, p_ref, a_ref):
        iq = iq_ref[e]
        k0 = pl.multiple_of(jk_ref[e] * tk, tk)
        v1 = jnp.concatenate([vT_ref[:, pl.ds(k0, tk)], jnp.ones((ONES_ROWS, tk), _BF16)], axis=0)
        acc_ref[iq] = a_ref[...] * acc_ref[iq] + _dot(v1, p_ref[...])

    def run(first, n_steps, masked):
        unroll = ATT_UNROLL if n_steps % ATT_UNROLL == 0 else 2
        for _, _, p_ref, a_ref in bufs:
            p_ref[...] = jnp.zeros(p_ref.shape, _BF16)
            a_ref[...] = jnp.ones(a_ref.shape, _F32)
        for d in range(ATT_LEAD):
            stage_a(first + d, *bufs[d % 2][:2], masked)

        def body(u, carry):
            for d in range(unroll):
                e = first + u * unroll + d
                s_ref, c_ref, p_ref, a_ref = bufs[d % 2]
                stage_c(e - ATT_LEAD, p_ref, a_ref)
                stage_b(e, s_ref, c_ref, p_ref, a_ref)
                stage_a(e + ATT_LEAD, s_ref, c_ref, masked)
            return carry

        lax.fori_loop(0, n_steps // unroll, body, 0)
        for d in range(ATT_LEAD):
            stage_c(first + n_steps - ATT_LEAD + d, *bufs[d % 2][2:])

    acc_ref[...] = jnp.zeros(acc_ref.shape, _F32)
    m_ref[...] = jnp.full(m_ref.shape, -jnp.inf, _F32)
    for (first, n_steps), masked in zip(seqs, (False, True)):
        if n_steps:
            run(first, n_steps, masked)
    for iq in range(nq):
        acc = acc_ref[iq]
        o_ref[:, iq * tq:(iq + 1) * tq] = (
            acc[0:V_HEAD_DIM] / acc[V_HEAD_DIM:V_HEAD_DIM + 1]).astype(_BF16)


def _attention(qT, k, vT):
    S = qT.shape[1]
    tq = min(ATT_TQ, S)
    tk = tq // 2
    nq = S // tq
    tables, seqs = _attn_tables(nq)
    assert ATT_LEAD == 2 and ATT_UNROLL % 2 == 0
    grid_spec = pltpu.PrefetchScalarGridSpec(
        num_scalar_prefetch=3,
        grid=(MLA_HEADS,),
        in_specs=[pl.BlockSpec((HEAD_PAD, S), lambda h, *_: (h, 0)),
                  pl.BlockSpec((None, S, HEAD_PAD), lambda h, *_: (h, 0, 0)),
                  pl.BlockSpec((V_HEAD_DIM, S), lambda h, *_: (h, 0))],
        out_specs=pl.BlockSpec((V_HEAD_DIM, S), lambda h, *_: (h, 0)),
        scratch_shapes=[pltpu.VMEM((tk, tq), _F32), pltpu.VMEM((tk, tq), _F32),
                        pltpu.VMEM((1, tq), _F32), pltpu.VMEM((1, tq), _F32),
                        pltpu.VMEM((tk, tq), _BF16), pltpu.VMEM((tk, tq), _BF16),
                        pltpu.VMEM((1, tq), _F32), pltpu.VMEM((1, tq), _F32),
                        pltpu.VMEM((nq, 1, tq), _F32),
                        pltpu.VMEM((nq, V_HEAD_DIM + ONES_ROWS, tq), _F32),
                        pltpu.VMEM((2, tk, tq), _F32)])
    return pl.pallas_call(
        functools.partial(_attn_body, seqs=tuple(seqs)),
        out_shape=jax.ShapeDtypeStruct((MLA_OUT, S), _BF16),
        grid_spec=grid_spec,
        compiler_params=_params(("arbitrary",)),
        name="mla_attention",
    )(*tables, qT, k, vT)


CONV_HALO = 8
CONV_TC = V7X_MXU_COLS


def _conv_body(x_ref, g_ref, win_ref, cw_ref, wout_ref, o_ref, h_ref, cz_ref, by_ref):
    tm = x_ref.shape[0]
    x = x_ref[...]
    h_ref[...] = _rms_rows(x, g_ref[...]).astype(_BF16)
    W = D_MODEL

    @pl.when(pl.program_id(0) == 0)
    def _():
        cz_ref[0:CONV_HALO, :] = jnp.zeros((CONV_HALO, W), _F32)

    for j in range(W // CONV_TC):
        cols = slice(j * CONV_TC, (j + 1) * CONV_TC)
        proj = [_dot(h_ref[...], win_ref[:, part * W + j * CONV_TC:part * W + (j + 1) * CONV_TC])
                for part in range(3)]
        b_gate, cz = proj[0], proj[1] * proj[2]
        cz_ref[CONV_HALO:CONV_HALO + tm, cols] = cz
        y = cz * cw_ref[2:3, cols]
        y = y + cz_ref[CONV_HALO - 1:CONV_HALO - 1 + tm, cols] * cw_ref[1:2, cols]
        y = y + cz_ref[CONV_HALO - 2:CONV_HALO - 2 + tm, cols] * cw_ref[0:1, cols]
        cz_ref[0:CONV_HALO, cols] = cz[tm - CONV_HALO:tm]
        by_ref[:, cols] = (b_gate * y).astype(_BF16)
    o_ref[...] = x + _dot(by_ref[...], wout_ref[...])


def _conv_mixer(x, g, win, cw, wout):
    S = x.shape[0]
    tm = min(PROJ_TM, S)
    row_spec = pl.BlockSpec((tm, D_MODEL), lambda i: (i, 0))
    return pl.pallas_call(
        _conv_body,
        out_shape=jax.ShapeDtypeStruct((S, D_MODEL), _F32),
        grid=(S // tm,),
        in_specs=[row_spec, _const_spec(g), _const_spec(win), _const_spec(cw),
                  _const_spec(wout)],
        out_specs=row_spec,
        scratch_shapes=[pltpu.VMEM((tm, D_MODEL), _BF16),
                        pltpu.VMEM((tm + CONV_HALO, D_MODEL), _F32),
                        pltpu.VMEM((tm, D_MODEL), _BF16)],
        compiler_params=_params(("arbitrary",)),
        name="conv_mixer",
    )(x, *map(_operand, (g, win, cw, wout)))


def _prep_even(w_in, w_uq, w_ukv, sg_w):
    n = w_in.shape[0]
    winT = jnp.swapaxes(w_in, 1, 2).astype(_BF16)
    wq = w_uq.reshape(n, Q_LORA_RANK, MLA_HEADS, QK_HEAD_DIM)
    wq = jnp.pad(wq, ((0, 0), (0, 0), (0, 0), (0, HEAD_PAD - QK_HEAD_DIM)))
    wuqT = jnp.swapaxes(wq.reshape(n, Q_LORA_RANK, MLA_HEADS * HEAD_PAD), 1, 2).astype(_BF16)
    wkv = w_ukv.reshape(n, KV_LORA_RANK, MLA_HEADS, QK_NOPE_DIM + V_HEAD_DIM)
    wuvT = jnp.swapaxes(wkv[..., QK_NOPE_DIM:].reshape(n, KV_LORA_RANK, MLA_OUT), 1, 2)
    wk_nope = jnp.pad(wkv[..., :QK_NOPE_DIM],
                      ((0, 0), (0, 0), (0, 0), (0, HEAD_PAD - QK_NOPE_DIM)))
    place = jnp.pad(jnp.eye(QK_ROPE_DIM, dtype=_F32),
                    ((0, 0), (QK_NOPE_DIM, HEAD_PAD - QK_HEAD_DIM)))
    place = jnp.broadcast_to(place[None, :, None, :], (n, QK_ROPE_DIM, MLA_HEADS, HEAD_PAD))
    wk = jnp.concatenate([wk_nope, place], axis=1).reshape(
        n, KV_LORA_RANK + QK_ROPE_DIM, MLA_HEADS * HEAD_PAD).astype(_BF16)
    sgwT = jnp.swapaxes(sg_w, 2, 3).reshape(n, SG_GROUPS * SG_CHUNK, SG_CHUNK)
    return winT, wuqT, wuvT.astype(_BF16), wk, sgwT


def kernel(x, positions, ffn_pre_norm, ffn_pre_w_gate, ffn_pre_w_up, ffn_pre_w_down, mix_norm, ffn_post_norm, ffn_post_w_gate, ffn_post_w_up, ffn_post_w_down, even_w_in, q_norm, w_uq, kv_norm, w_ukv, sg_norm, sg_w, sg_b, even_w_out, conv_w_in, conv_w, conv_w_out, final_norm):
    B, S, _ = x.shape
    assert B == 1 and S % SG_CHUNK == 0
    xs = x[0]
    inv_freq = ROPE_THETA ** (-jnp.arange(0, QK_ROPE_DIM, 2, dtype=_F32) / QK_ROPE_DIM)
    ang = positions[0].astype(_F32)[None, :] * inv_freq[:, None]
    cosT, sinT = jnp.cos(ang), jnp.sin(ang)
    bf = lambda w: w.astype(_BF16)
    rows = lambda v: v[:, None, :]
    cols = lambda v: v[:, :, None]

    pre_w = (ffn_pre_w_gate, ffn_pre_w_up, ffn_pre_w_down)
    post_w = (ffn_post_w_gate, ffn_post_w_up, ffn_post_w_down)
    pre_g, post_g, mix_g = rows(ffn_pre_norm), rows(ffn_post_norm), rows(mix_norm)
    w_bf16 = tuple(bf(w[0]) for w in pre_w)
    winT, wuqT, wuvT, wk, sgwT = _prep_even(even_w_in, w_uq, w_ukv, sg_w)
    qn, kvn, sgn, wo = cols(q_norm), cols(kv_norm), cols(sg_norm), bf(even_w_out)
    conv = (bf(conv_w_in), conv_w, bf(conv_w_out))

    for layer in range(DEPTH):
        xs, w_bf16 = _ffn(xs, _Slab(pre_g, layer), *w_bf16, nxt=(layer, *post_w))
        mix = None
        if layer % 2 == 0:
            e = layer // 2
            sl = lambda w: _Slab(w, e)
            qT, k, vT, sgT = _even_proj(
                xs, _Slab(mix_g, layer), sl(winT), sl(qn), sl(wuqT), sl(kvn), sl(wuvT), sl(wk),
                sl(sgn), sl(sgwT), sl(sg_b), cosT, sinT)
            attnT = _attention(qT, k, vT)
            mix = (attnT, sgT, _Slab(wo, e, 0, 2), _Slab(wo, e, 1, 2))
        else:
            o = layer // 2
            xs = _conv_mixer(xs, _Slab(mix_g, layer), *(_Slab(w, o) for w in conv))
        last = layer == DEPTH - 1
        xs, w_bf16 = _ffn(xs, _Slab(post_g, layer), *w_bf16, mix=mix,
                          final_g=final_norm.reshape(1, -1) if last else None,
                          nxt=None if last else (layer + 1, *pre_w))
    return xs[None]
```

```python
import functools
import math
from typing import NamedTuple

import jax
import jax.numpy as jnp
from jax import lax
from jax.experimental import pallas as pl
from jax.experimental.pallas import tpu as pltpu

D_MODEL = 1024
DEPTH = 4
MLA_HEADS = 8
QK_NOPE_DIM = 64
QK_ROPE_DIM = 32
V_HEAD_DIM = 64
Q_LORA_RANK = 384
KV_LORA_RANK = 256
ROPE_THETA = 10000.0
SG_GROUPS = 8
SG_GROUP_DIM = 64
SG_WIDTH = SG_GROUPS * SG_GROUP_DIM
SG_CHUNK = 128
CONV_K = 3
D_FF = 2816
NORM_EPS = 1e-6
QK_HEAD_DIM = QK_NOPE_DIM + QK_ROPE_DIM
MLA_OUT = MLA_HEADS * V_HEAD_DIM
EVEN_IN = Q_LORA_RANK + KV_LORA_RANK + QK_ROPE_DIM + 2 * SG_WIDTH
HALF_ROPE = QK_ROPE_DIM // 2

V7X_LANES = 128
V7X_MXU_COLS = 256
V7X_VMEM_BYTES = 64 * 1024 * 1024

HEAD_PAD = V7X_LANES
FFN_TM = 1024
FFN_TF = V7X_MXU_COLS
FFN_CAST_BANDS = 16
PROJ_TM = 1024
ATT_TQ = 1024
VMEM_LIMIT = 56 * 1024 * 1024

_NT = (((1,), (1,)), ((), ()))
_TN = (((0,), (0,)), ((), ()))
_F32 = jnp.float32
_BF16 = jnp.bfloat16


def _dot(a, b, dims=None):
    if dims is None:
        return jnp.dot(a, b, preferred_element_type=_F32)
    return lax.dot_general(a, b, dims, preferred_element_type=_F32)


def _rms_rows(x, g):
    ms = jnp.mean(x * x, axis=-1, keepdims=True)
    return x * lax.rsqrt(ms + NORM_EPS) * g


def _rms_cols(xT, g):
    ms = jnp.mean(xT * xT, axis=0, keepdims=True)
    return xT * lax.rsqrt(ms + NORM_EPS) * g


class _Slab(NamedTuple):
    stack: jax.Array
    index: int
    part: int = 0
    parts: int = 1


def _operand(p):
    return p.stack if isinstance(p, _Slab) else p


def _const_spec(p):
    if isinstance(p, _Slab):
        _, rows, cols = p.stack.shape
        return pl.BlockSpec((None, rows // p.parts, cols), lambda *_: (p.index, p.part, 0),
                            pipeline_mode=pl.Buffered(1))
    n = p.ndim
    return pl.BlockSpec(p.shape, lambda *_: (0,) * n, pipeline_mode=pl.Buffered(1))


def _params(semantics, flags=None):
    return pltpu.CompilerParams(dimension_semantics=semantics,
                                vmem_limit_bytes=VMEM_LIMIT, flags=flags)


def _ffn_body(has_mix, has_final, has_next, *refs):
    it = iter(refs)
    x_ref = next(it)
    if has_mix:
        attn_ref, sg_ref, woa_ref, wos_ref = next(it), next(it), next(it), next(it)
    g_ref, wg_ref, wu_ref, wd_ref = next(it), next(it), next(it), next(it)
    gf_ref = next(it) if has_final else None
    next_f32 = [next(it) for _ in range(3)] if has_next else []
    o_ref = next(it)
    next_bf16 = [next(it) for _ in range(3)] if has_next else []
    h_ref, a_ref = next(it), next(it)

    for src_ref, dst_ref in zip(next_f32, next_bf16):
        dst_ref[...] = src_ref[...].astype(_BF16)

    x = x_ref[...]
    if has_mix:
        x = x + _dot(attn_ref[...], woa_ref[...], _TN) + _dot(sg_ref[...], wos_ref[...], _TN)
    h_ref[...] = _rms_rows(x, g_ref[...]).astype(_BF16)
    for j in range(D_FF // FFN_TF):
        cols = slice(j * FFN_TF, (j + 1) * FFN_TF)
        gate = _dot(h_ref[...], wg_ref[:, cols])
        up = _dot(h_ref[...], wu_ref[:, cols])
        a_ref[:, cols] = (gate * jax.nn.sigmoid(gate) * up).astype(_BF16)
    out = x + 0.5 * _dot(a_ref[...], wd_ref[...])
    if has_final:
        out = _rms_rows(out, gf_ref[...])
    o_ref[...] = out


def _ffn(x, g, wg, wu, wd, mix=None, final_g=None, nxt=None):
    S = x.shape[0]
    tm = min(FFN_TM, S)
    steps = S // tm
    row_spec = pl.BlockSpec((tm, D_MODEL), lambda i: (i, 0))
    args, specs = [x], [row_spec]
    out_shape, out_specs = [jax.ShapeDtypeStruct((S, D_MODEL), _F32)], [row_spec]
    if mix is not None:
        attnT, sgT, woa, wos = mix
        colT = pl.BlockSpec((MLA_OUT, tm), lambda i: (0, i))
        args += [attnT, sgT, woa, wos]
        specs += [colT, colT, _const_spec(woa), _const_spec(wos)]
    args += [g, wg, wu, wd]
    specs += [_const_spec(g), _const_spec(wg), _const_spec(wu), _const_spec(wd)]
    if final_g is not None:
        args.append(final_g)
        specs.append(_const_spec(final_g))
    if nxt is not None:
        layer, *stacks = nxt
        bands = math.gcd(steps, FFN_CAST_BANDS)
        for w in stacks:
            _, rows, cols = w.shape
            args.append(w)
            specs.append(pl.BlockSpec((None, rows // bands, cols),
                                      lambda i: (layer, i * bands // steps, 0)))
            out_shape.append(jax.ShapeDtypeStruct((rows, cols), _BF16))
            out_specs.append(pl.BlockSpec((rows // bands, cols),
                                          lambda i: (i * bands // steps, 0)))
    outs = pl.pallas_call(
        functools.partial(_ffn_body, mix is not None, final_g is not None, nxt is not None),
        out_shape=tuple(out_shape),
        grid=(steps,),
        in_specs=specs,
        out_specs=tuple(out_specs),
        scratch_shapes=[pltpu.VMEM((tm, D_MODEL), _BF16), pltpu.VMEM((tm, D_FF), _BF16)],
        compiler_params=_params(("arbitrary",)),
        name="ffn_mix" if mix is not None else ("ffn_final" if final_g is not None else "ffn"),
    )(*map(_operand, args))
    return outs[0], tuple(outs[1:])


def _even_proj_body(x_ref, g_ref, winT_ref, qn_ref, wuqT_ref, kvn_ref, wuvT_ref,
                    wk_ref, sgn_ref, sgwT_ref, sgb_ref, cos_ref, sin_ref,
                    qT_ref, k_ref, vT_ref, sgT_ref):
    tm = x_ref.shape[0]
    h = _rms_rows(x_ref[...], g_ref[...]).astype(_BF16)
    projT = _dot(winT_ref[...], h, _NT)
    o_kv = Q_LORA_RANK
    o_kr = o_kv + KV_LORA_RANK
    o_z = o_kr + QK_ROPE_DIM
    cos = cos_ref[...]
    sin = sin_ref[...]

    cqn = _rms_cols(projT[0:o_kv], qn_ref[...]).astype(_BF16)
    qT = _dot(wuqT_ref[...], cqn) * (QK_HEAD_DIM ** -0.5 * math.log2(math.e))
    pieces = []
    for hd in range(MLA_HEADS):
        b = hd * HEAD_PAD
        t1 = qT[b + QK_NOPE_DIM:b + QK_NOPE_DIM + HALF_ROPE]
        t2 = qT[b + QK_NOPE_DIM + HALF_ROPE:b + QK_HEAD_DIM]
        pieces += [qT[b:b + QK_NOPE_DIM], t1 * cos - t2 * sin, t1 * sin + t2 * cos,
                   qT[b + QK_HEAD_DIM:b + HEAD_PAD]]
    qT_ref[...] = jnp.concatenate(pieces, axis=0).astype(_BF16)

    ckvn = _rms_cols(projT[o_kv:o_kr], kvn_ref[...]).astype(_BF16)
    vT_ref[...] = _dot(wuvT_ref[...], ckvn).astype(_BF16)
    k1 = projT[o_kr:o_kr + HALF_ROPE]
    k2 = projT[o_kr + HALF_ROPE:o_z]
    kro = jnp.concatenate([k1 * cos - k2 * sin, k1 * sin + k2 * cos], axis=0)
    lhsT = jnp.concatenate([ckvn, kro.astype(_BF16)], axis=0)
    k_all = _dot(lhsT, wk_ref[...], _TN).astype(_BF16)
    for hd in range(MLA_HEADS):
        k_ref[hd] = k_all[:, hd * HEAD_PAD:(hd + 1) * HEAD_PAD]

    z = projT[o_z:EVEN_IN]
    z = 0.5 * z * (1.0 + lax.erf(z * (2.0 ** -0.5)))
    uT = z[0:SG_WIDTH]
    vn = _rms_cols(z[SG_WIDTH:2 * SG_WIDTH], sgn_ref[...]).astype(_BF16)
    nchunk = tm // SG_CHUNK
    s_idx = lax.broadcasted_iota(jnp.int32, (SG_CHUNK, SG_CHUNK), 0)
    t_idx = lax.broadcasted_iota(jnp.int32, (SG_CHUNK, SG_CHUNK), 1)
    causal = s_idx <= t_idx
    rows = []
    for gi in range(SG_GROUPS):
        r0 = gi * SG_GROUP_DIM
        lhs = jnp.concatenate(
            [vn[r0:r0 + SG_GROUP_DIM, n * SG_CHUNK:(n + 1) * SG_CHUNK] for n in range(nchunk)],
            axis=0)
        wT = jnp.where(causal, sgwT_ref[gi * SG_CHUNK:(gi + 1) * SG_CHUNK, :], 0.0)
        wT = wT.astype(_BF16)
        mixed = _dot(lhs, wT) + sgb_ref[gi:gi + 1, :]
        rows.append(jnp.concatenate(
            [mixed[n * SG_GROUP_DIM:(n + 1) * SG_GROUP_DIM] for n in range(nchunk)], axis=1))
    sgT_ref[...] = (uT * jnp.concatenate(rows, axis=0)).astype(_BF16)


def _even_proj(x, g, winT, qn, wuqT, kvn, wuvT, wk, sgn, sgwT, sgb, cosT, sinT):
    S = x.shape[0]
    tm = min(PROJ_TM, S)
    colT = lambda rows: pl.BlockSpec((rows, tm), lambda i: (0, i))
    consts = [g, winT, qn, wuqT, kvn, wuvT, wk, sgn, sgwT, sgb]
    return pl.pallas_call(
        _even_proj_body,
        out_shape=(jax.ShapeDtypeStruct((MLA_HEADS * HEAD_PAD, S), _BF16),
                   jax.ShapeDtypeStruct((MLA_HEADS, S, HEAD_PAD), _BF16),
                   jax.ShapeDtypeStruct((MLA_OUT, S), _BF16),
                   jax.ShapeDtypeStruct((SG_WIDTH, S), _BF16)),
        grid=(S // tm,),
        in_specs=[pl.BlockSpec((tm, D_MODEL), lambda i: (i, 0))]
        + [_const_spec(c) for c in consts]
        + [colT(HALF_ROPE), colT(HALF_ROPE)],
        out_specs=(colT(MLA_HEADS * HEAD_PAD),
                   pl.BlockSpec((MLA_HEADS, tm, HEAD_PAD), lambda i: (0, i, 0)),
                   colT(MLA_OUT), colT(SG_WIDTH)),
        compiler_params=_params(("arbitrary",)),
        name="even_proj",
    )(x, *map(_operand, consts), cosT, sinT)


ONES_ROWS = 16


ATT_LEAD = 2
ATT_UNROLL = 8


def _attn_tables(nq):
    below = [(iq, jk) for iq in range(nq) for jk in range(2 * iq)]
    diag = [(iq, 2 * iq + d) for iq in range(nq) for d in range(2)]
    rows, seqs = [], []
    for seq in (below, diag):
        if seq:
            seqs.append((len(rows) + ATT_LEAD, len(seq)))
            rows += [(0, 0)] * ATT_LEAD + seq + [seq[-1]] * ATT_LEAD
        else:
            seqs.append((0, 0))
    return tuple(jnp.asarray([r[c] for r in rows], jnp.int32) for c in range(2)), seqs


def _attn_body(iq_ref, jk_ref, qT_ref, k_ref, vT_ref, o_ref,
               s0_ref, s1_ref, c0_ref, c1_ref, p0_ref, p1_ref, a0_ref, a1_ref,
               m_ref, acc_ref, bias_ref, *, seqs):
    tk, tq = s0_ref.shape
    nq = acc_ref.shape[0]
    bufs = ((s0_ref, c0_ref, p0_ref, a0_ref), (s1_ref, c1_ref, p1_ref, a1_ref))

    @pl.when(pl.program_id(0) == 0)
    def _():
        r = lax.broadcasted_iota(jnp.int32, (tk, tq), 0)
        q = lax.broadcasted_iota(jnp.int32, (tk, tq), 1)
        bias_ref[0] = jnp.where(r <= q, 0.0, -jnp.inf)
        bias_ref[1] = jnp.where(r + tk <= q, 0.0, -jnp.inf)

    def stage_a(e, s_ref, c_ref, kind):
        lo = 0 if kind != 1 else tq // 2
        k0 = pl.multiple_of(jk_ref[e] * tk, tk)
        q0 = pl.multiple_of(iq_ref[e] * tq + lo, tq // 2)
        s = _dot(k_ref[pl.ds(k0, tk), :], qT_ref[:, pl.ds(q0, tq - lo)])
        if kind is not None:
            s = s + bias_ref[kind, :, lo:tq]
        s_ref[:, lo:tq] = s
        c_ref[:, lo:tq] = jnp.max(s, axis=0, keepdims=True)

    def stage_b(e, s_ref, c_ref, p_ref, a_ref, lo):
        iq = iq_ref[e]
        m_old = m_ref[iq, :, lo:tq]
        m_new = jnp.maximum(m_old, c_ref[:, lo:tq])
        m_ref[iq, :, lo:tq] = m_new
        a_ref[:, lo:tq] = jnp.exp2(m_old - m_new)
        p_ref[:, lo:tq] = jnp.exp2(s_ref[:, lo:tq] - m_new).astype(_BF16)

    def stage_c(e, p_ref, a_ref, lo):
        iq = iq_ref[e]
        k0 = pl.multiple_of(jk_ref[e] * tk, tk)
        v1 = jnp.concatenate([vT_ref[:, pl.ds(k0, tk)], jnp.ones((ONES_ROWS, tk), _BF16)], axis=0)
        acc_ref[iq, :, lo:tq] = (a_ref[:, lo:tq] * acc_ref[iq, :, lo:tq]
                                 + _dot(v1, p_ref[:, lo:tq]))

    def run(first, n_steps, masked):
        unroll = ATT_UNROLL if n_steps % ATT_UNROLL == 0 else 2
        kind = lambda d: d % 2 if masked else None
        lo = lambda d: tq // 2 if masked and d % 2 else 0
        for _, _, p_ref, a_ref in bufs:
            p_ref[...] = jnp.zeros(p_ref.shape, _BF16)
            a_ref[...] = jnp.ones(a_ref.shape, _F32)
        for d in range(ATT_LEAD):
            stage_a(first + d, *bufs[d % 2][:2], kind(d))

        def body(u, carry):
            for d in range(unroll):
                e = first + u * unroll + d
                s_ref, c_ref, p_ref, a_ref = bufs[d % 2]
                stage_c(e - ATT_LEAD, p_ref, a_ref, lo(d))
                stage_b(e, s_ref, c_ref, p_ref, a_ref, lo(d))
                stage_a(e + ATT_LEAD, s_ref, c_ref, kind(d))
            return carry

        lax.fori_loop(0, n_steps // unroll, body, 0)
        for d in range(ATT_LEAD):
            stage_c(first + n_steps - ATT_LEAD + d, *bufs[d % 2][2:], lo(d))

    acc_ref[...] = jnp.zeros(acc_ref.shape, _F32)
    m_ref[...] = jnp.full(m_ref.shape, -jnp.inf, _F32)
    for (first, n_steps), masked in zip(seqs, (False, True)):
        if n_steps:
            run(first, n_steps, masked)
    for iq in range(nq):
        acc = acc_ref[iq]
        o_ref[:, iq * tq:(iq + 1) * tq] = (
            acc[0:V_HEAD_DIM] / acc[V_HEAD_DIM:V_HEAD_DIM + 1]).astype(_BF16)


def _attention(qT, k, vT):
    S = qT.shape[1]
    tq = min(ATT_TQ, S)
    tk = tq // 2
    nq = S // tq
    tables, seqs = _attn_tables(nq)
    assert ATT_LEAD == 2 and ATT_UNROLL % 2 == 0
    grid_spec = pltpu.PrefetchScalarGridSpec(
        num_scalar_prefetch=2,
        grid=(MLA_HEADS,),
        in_specs=[pl.BlockSpec((HEAD_PAD, S), lambda h, *_: (h, 0)),
                  pl.BlockSpec((None, S, HEAD_PAD), lambda h, *_: (h, 0, 0)),
                  pl.BlockSpec((V_HEAD_DIM, S), lambda h, *_: (h, 0))],
        out_specs=pl.BlockSpec((V_HEAD_DIM, S), lambda h, *_: (h, 0)),
        scratch_shapes=[pltpu.VMEM((tk, tq), _F32), pltpu.VMEM((tk, tq), _F32),
                        pltpu.VMEM((1, tq), _F32), pltpu.VMEM((1, tq), _F32),
                        pltpu.VMEM((tk, tq), _BF16), pltpu.VMEM((tk, tq), _BF16),
                        pltpu.VMEM((1, tq), _F32), pltpu.VMEM((1, tq), _F32),
                        pltpu.VMEM((nq, 1, tq), _F32),
                        pltpu.VMEM((nq, V_HEAD_DIM + ONES_ROWS, tq), _F32),
                        pltpu.VMEM((2, tk, tq), _F32)])
    return pl.pallas_call(
        functools.partial(_attn_body, seqs=tuple(seqs)),
        out_shape=jax.ShapeDtypeStruct((MLA_OUT, S), _BF16),
        grid_spec=grid_spec,
        compiler_params=_params(("arbitrary",)),
        name="mla_attention",
    )(*tables, qT, k, vT)


CONV_HALO = 8
CONV_TC = V7X_MXU_COLS


def _conv_body(x_ref, g_ref, win_ref, cw_ref, wout_ref, o_ref, h_ref, cz_ref, by_ref):
    tm = x_ref.shape[0]
    x = x_ref[...]
    h_ref[...] = _rms_rows(x, g_ref[...]).astype(_BF16)
    W = D_MODEL

    @pl.when(pl.program_id(0) == 0)
    def _():
        cz_ref[0:CONV_HALO, :] = jnp.zeros((CONV_HALO, W), _F32)

    for j in range(W // CONV_TC):
        cols = slice(j * CONV_TC, (j + 1) * CONV_TC)
        proj = [_dot(h_ref[...], win_ref[:, part * W + j * CONV_TC:part * W + (j + 1) * CONV_TC])
                for part in range(3)]
        b_gate, cz = proj[0], proj[1] * proj[2]
        cz_ref[CONV_HALO:CONV_HALO + tm, cols] = cz
        y = cz * cw_ref[2:3, cols]
        y = y + cz_ref[CONV_HALO - 1:CONV_HALO - 1 + tm, cols] * cw_ref[1:2, cols]
        y = y + cz_ref[CONV_HALO - 2:CONV_HALO - 2 + tm, cols] * cw_ref[0:1, cols]
        cz_ref[0:CONV_HALO, cols] = cz[tm - CONV_HALO:tm]
        by_ref[:, cols] = (b_gate * y).astype(_BF16)
    o_ref[...] = x + _dot(by_ref[...], wout_ref[...])


def _conv_mixer(x, g, win, cw, wout):
    S = x.shape[0]
    tm = min(PROJ_TM, S)
    row_spec = pl.BlockSpec((tm, D_MODEL), lambda i: (i, 0))
    return pl.pallas_call(
        _conv_body,
        out_shape=jax.ShapeDtypeStruct((S, D_MODEL), _F32),
        grid=(S // tm,),
        in_specs=[row_spec, _const_spec(g), _const_spec(win), _const_spec(cw),
                  _const_spec(wout)],
        out_specs=row_spec,
        scratch_shapes=[pltpu.VMEM((tm, D_MODEL), _BF16),
                        pltpu.VMEM((tm + CONV_HALO, D_MODEL), _F32),
                        pltpu.VMEM((tm, D_MODEL), _BF16)],
        compiler_params=_params(("arbitrary",)),
        name="conv_mixer",
    )(x, *map(_operand, (g, win, cw, wout)))


def _prep_even(w_in, w_uq, w_ukv, sg_w):
    n = w_in.shape[0]
    winT = jnp.swapaxes(w_in, 1, 2).astype(_BF16)
    wq = w_uq.reshape(n, Q_LORA_RANK, MLA_HEADS, QK_HEAD_DIM)
    wq = jnp.pad(wq, ((0, 0), (0, 0), (0, 0), (0, HEAD_PAD - QK_HEAD_DIM)))
    wuqT = jnp.swapaxes(wq.reshape(n, Q_LORA_RANK, MLA_HEADS * HEAD_PAD), 1, 2).astype(_BF16)
    wkv = w_ukv.reshape(n, KV_LORA_RANK, MLA_HEADS, QK_NOPE_DIM + V_HEAD_DIM)
    wuvT = jnp.swapaxes(wkv[..., QK_NOPE_DIM:].reshape(n, KV_LORA_RANK, MLA_OUT), 1, 2)
    wk_nope = jnp.pad(wkv[..., :QK_NOPE_DIM],
                      ((0, 0), (0, 0), (0, 0), (0, HEAD_PAD - QK_NOPE_DIM)))
    place = jnp.pad(jnp.eye(QK_ROPE_DIM, dtype=_F32),
                    ((0, 0), (QK_NOPE_DIM, HEAD_PAD - QK_HEAD_DIM)))
    place = jnp.broadcast_to(place[None, :, None, :], (n, QK_ROPE_DIM, MLA_HEADS, HEAD_PAD))
    wk = jnp.concatenate([wk_nope, place], axis=1).reshape(
        n, KV_LORA_RANK + QK_ROPE_DIM, MLA_HEADS * HEAD_PAD).astype(_BF16)
    sgwT = jnp.swapaxes(sg_w, 2, 3).reshape(n, SG_GROUPS * SG_CHUNK, SG_CHUNK)
    return winT, wuqT, wuvT.astype(_BF16), wk, sgwT


def kernel(x, positions, ffn_pre_norm, ffn_pre_w_gate, ffn_pre_w_up, ffn_pre_w_down, mix_norm, ffn_post_norm, ffn_post_w_gate, ffn_post_w_up, ffn_post_w_down, even_w_in, q_norm, w_uq, kv_norm, w_ukv, sg_norm, sg_w, sg_b, even_w_out, conv_w_in, conv_w, conv_w_out, final_norm):
    B, S, _ = x.shape
    assert B == 1 and S % SG_CHUNK == 0
    xs = x[0]
    inv_freq = ROPE_THETA ** (-jnp.arange(0, QK_ROPE_DIM, 2, dtype=_F32) / QK_ROPE_DIM)
    ang = positions[0].astype(_F32)[None, :] * inv_freq[:, None]
    cosT, sinT = jnp.cos(ang), jnp.sin(ang)
    bf = lambda w: w.astype(_BF16)
    rows = lambda v: v[:, None, :]
    cols = lambda v: v[:, :, None]

    pre_w = (ffn_pre_w_gate, ffn_pre_w_up, ffn_pre_w_down)
    post_w = (ffn_post_w_gate, ffn_post_w_up, ffn_post_w_down)
    pre_g, post_g, mix_g = rows(ffn_pre_norm), rows(ffn_post_norm), rows(mix_norm)
    w_bf16 = tuple(bf(w[0]) for w in pre_w)
    winT, wuqT, wuvT, wk, sgwT = _prep_even(even_w_in, w_uq, w_ukv, sg_w)
    qn, kvn, sgn, wo = cols(q_norm), cols(kv_norm), cols(sg_norm), bf(even_w_out)
    conv = (bf(conv_w_in), conv_w, bf(conv_w_out))

    for layer in range(DEPTH):
        xs, w_bf16 = _ffn(xs, _Slab(pre_g, layer), *w_bf16, nxt=(layer, *post_w))
        mix = None
        if layer % 2 == 0:
            e = layer // 2
            sl = lambda w: _Slab(w, e)
            qT, k, vT, sgT = _even_proj(
                xs, _Slab(mix_g, layer), sl(winT), sl(qn), sl(wuqT), sl(kvn), sl(wuvT), sl(wk),
                sl(sgn), sl(sgwT), sl(sg_b), cosT, sinT)
            attnT = _attention(qT, k, vT)
            mix = (attnT, sgT, _Slab(wo, e, 0, 2), _Slab(wo, e, 1, 2))
        else:
            o = layer // 2
            xs = _conv_mixer(xs, _Slab(mix_g, layer), *(_Slab(w, o) for w in conv))
        last = layer == DEPTH - 1
        xs, w_bf16 = _ffn(xs, _Slab(post_g, layer), *w_bf16, mix=mix,
                          final_g=final_norm.reshape(1, -1) if last else None,
                          nxt=None if last else (layer + 1, *pre_w))
    return xs[None]
```

```python
import functools
import math
from typing import NamedTuple

import jax
import jax.numpy as jnp
from jax import lax
from jax.experimental import pallas as pl
from jax.experimental.pallas import tpu as pltpu

D_MODEL = 1024
DEPTH = 4
MLA_HEADS = 8
QK_NOPE_DIM = 64
QK_ROPE_DIM = 32
V_HEAD_DIM = 64
Q_LORA_RANK = 384
KV_LORA_RANK = 256
ROPE_THETA = 10000.0
SG_GROUPS = 8
SG_GROUP_DIM = 64
SG_WIDTH = SG_GROUPS * SG_GROUP_DIM
SG_CHUNK = 128
CONV_K = 3
D_FF = 2816
NORM_EPS = 1e-6
QK_HEAD_DIM = QK_NOPE_DIM + QK_ROPE_DIM
MLA_OUT = MLA_HEADS * V_HEAD_DIM
EVEN_IN = Q_LORA_RANK + KV_LORA_RANK + QK_ROPE_DIM + 2 * SG_WIDTH
HALF_ROPE = QK_ROPE_DIM // 2

V7X_LANES = 128
V7X_MXU_COLS = 256
V7X_VMEM_BYTES = 64 * 1024 * 1024

HEAD_PAD = V7X_LANES
FFN_TM = 1024
FFN_TF = V7X_MXU_COLS
FFN_CAST_BANDS = 16
PROJ_TM = 1024
ATT_TQ = 1024
VMEM_LIMIT = 56 * 1024 * 1024

_NT = (((1,), (1,)), ((), ()))
_TN = (((0,), (0,)), ((), ()))
_F32 = jnp.float32
_BF16 = jnp.bfloat16


def _dot(a, b, dims=None):
    if dims is None:
        return jnp.dot(a, b, preferred_element_type=_F32)
    return lax.dot_general(a, b, dims, preferred_element_type=_F32)


def _rms_rows(x, g):
    ms = jnp.mean(x * x, axis=-1, keepdims=True)
    return x * lax.rsqrt(ms + NORM_EPS) * g


def _rms_cols(xT, g):
    ms = jnp.mean(xT * xT, axis=0, keepdims=True)
    return xT * lax.rsqrt(ms + NORM_EPS) * g


class _Slab(NamedTuple):
    stack: jax.Array
    index: int
    part: int = 0
    parts: int = 1


def _operand(p):
    return p.stack if isinstance(p, _Slab) else p


def _const_spec(p):
    if isinstance(p, _Slab):
        _, rows, cols = p.stack.shape
        return pl.BlockSpec((None, rows // p.parts, cols), lambda *_: (p.index, p.part, 0),
                            pipeline_mode=pl.Buffered(1))
    n = p.ndim
    return pl.BlockSpec(p.shape, lambda *_: (0,) * n, pipeline_mode=pl.Buffered(1))


def _params(semantics, flags=None):
    return pltpu.CompilerParams(dimension_semantics=semantics,
                                vmem_limit_bytes=VMEM_LIMIT, flags=flags)


def _ffn_body(has_mix, has_final, has_next, *refs):
    it = iter(refs)
    x_ref = next(it)
    if has_mix:
        attn_ref, sg_ref, woa_ref, wos_ref = next(it), next(it), next(it), next(it)
    g_ref, wg_ref, wu_ref, wd_ref = next(it), next(it), next(it), next(it)
    gf_ref = next(it) if has_final else None
    next_f32 = [next(it) for _ in range(3)] if has_next else []
    o_ref = next(it)
    next_bf16 = [next(it) for _ in range(3)] if has_next else []
    h_ref, a_ref = next(it), next(it)

    for src_ref, dst_ref in zip(next_f32, next_bf16):
        dst_ref[...] = src_ref[...].astype(_BF16)

    x = x_ref[...]
    if has_mix:
        x = x + _dot(attn_ref[...], woa_ref[...], _TN) + _dot(sg_ref[...], wos_ref[...], _TN)
    h_ref[...] = _rms_rows(x, g_ref[...]).astype(_BF16)
    for j in range(D_FF // FFN_TF):
        cols = slice(j * FFN_TF, (j + 1) * FFN_TF)
        gate = _dot(h_ref[...], wg_ref[:, cols])
        up = _dot(h_ref[...], wu_ref[:, cols])
        a_ref[:, cols] = (gate * jax.nn.sigmoid(gate) * up).astype(_BF16)
    out = x + 0.5 * _dot(a_ref[...], wd_ref[...])
    if has_final:
        out = _rms_rows(out, gf_ref[...])
    o_ref[...] = out


def _ffn(x, g, wg, wu, wd, mix=None, final_g=None, nxt=None):
    S = x.shape[0]
    tm = min(FFN_TM, S)
    steps = S // tm
    row_spec = pl.BlockSpec((tm, D_MODEL), lambda i: (i, 0))
    args, specs = [x], [row_spec]
    out_shape, out_specs = [jax.ShapeDtypeStruct((S, D_MODEL), _F32)], [row_spec]
    if mix is not None:
        attnT, sgT, woa, wos = mix
        colT = pl.BlockSpec((MLA_OUT, tm), lambda i: (0, i))
        args += [attnT, sgT, woa, wos]
        specs += [colT, colT, _const_spec(woa), _const_spec(wos)]
    args += [g, wg, wu, wd]
    specs += [_const_spec(g), _const_spec(wg), _const_spec(wu), _const_spec(wd)]
    if final_g is not None:
        args.append(final_g)
        specs.append(_const_spec(final_g))
    if nxt is not None:
        layer, *stacks = nxt
        bands = math.gcd(steps, FFN_CAST_BANDS)
        for w in stacks:
            _, rows, cols = w.shape
            args.append(w)
            specs.append(pl.BlockSpec((None, rows // bands, cols),
                                      lambda i: (layer, i * bands // steps, 0)))
            out_shape.append(jax.ShapeDtypeStruct((rows, cols), _BF16))
            out_specs.append(pl.BlockSpec((rows // bands, cols),
                                          lambda i: (i * bands // steps, 0)))
    outs = pl.pallas_call(
        functools.partial(_ffn_body, mix is not None, final_g is not None, nxt is not None),
        out_shape=tuple(out_shape),
        grid=(steps,),
        in_specs=specs,
        out_specs=tuple(out_specs),
        scratch_shapes=[pltpu.VMEM((tm, D_MODEL), _BF16), pltpu.VMEM((tm, D_FF), _BF16)],
        compiler_params=_params(("arbitrary",)),
        name="ffn_mix" if mix is not None else ("ffn_final" if final_g is not None else "ffn"),
    )(*map(_operand, args))
    return outs[0], tuple(outs[1:])


def _even_proj_body(x_ref, g_ref, winT_ref, qn_ref, wuqT_ref, kvn_ref, wuvT_ref,
                    wk_ref, sgn_ref, sgwT_ref, sgb_ref, cos_ref, sin_ref,
                    qT_ref, k_ref, vT_ref, sgT_ref):
    tm = x_ref.shape[0]
    h = _rms_rows(x_ref[...], g_ref[...]).astype(_BF16)
    projT = _dot(winT_ref[...], h, _NT)
    o_kv = Q_LORA_RANK
    o_kr = o_kv + KV_LORA_RANK
    o_z = o_kr + QK_ROPE_DIM
    cos = cos_ref[...]
    sin = sin_ref[...]

    cqn = _rms_cols(projT[0:o_kv], qn_ref[...]).astype(_BF16)
    qT = _dot(wuqT_ref[...], cqn) * (QK_HEAD_DIM ** -0.5 * math.log2(math.e))
    pieces = []
    for hd in range(MLA_HEADS):
        b = hd * HEAD_PAD
        t1 = qT[b + QK_NOPE_DIM:b + QK_NOPE_DIM + HALF_ROPE]
        t2 = qT[b + QK_NOPE_DIM + HALF_ROPE:b + QK_HEAD_DIM]
        pieces += [qT[b:b + QK_NOPE_DIM], t1 * cos - t2 * sin, t1 * sin + t2 * cos,
                   qT[b + QK_HEAD_DIM:b + HEAD_PAD]]
    qT_ref[...] = jnp.concatenate(pieces, axis=0).astype(_BF16)

    ckvn = _rms_cols(projT[o_kv:o_kr], kvn_ref[...]).astype(_BF16)
    vT_ref[...] = _dot(wuvT_ref[...], ckvn).astype(_BF16)
    k1 = projT[o_kr:o_kr + HALF_ROPE]
    k2 = projT[o_kr + HALF_ROPE:o_z]
    kro = jnp.concatenate([k1 * cos - k2 * sin, k1 * sin + k2 * cos], axis=0)
    lhsT = jnp.concatenate([ckvn, kro.astype(_BF16)], axis=0)
    k_all = _dot(lhsT, wk_ref[...], _TN).astype(_BF16)
    for hd in range(MLA_HEADS):
        k_ref[hd] = k_all[:, hd * HEAD_PAD:(hd + 1) * HEAD_PAD]

    z = projT[o_z:EVEN_IN]
    z = 0.5 * z * (1.0 + lax.erf(z * (2.0 ** -0.5)))
    uT = z[0:SG_WIDTH]
    vn = _rms_cols(z[SG_WIDTH:2 * SG_WIDTH], sgn_ref[...]).astype(_BF16)
    nchunk = tm // SG_CHUNK
    s_idx = lax.broadcasted_iota(jnp.int32, (SG_CHUNK, SG_CHUNK), 0)
    t_idx = lax.broadcasted_iota(jnp.int32, (SG_CHUNK, SG_CHUNK), 1)
    causal = s_idx <= t_idx
    rows = []
    for gi in range(SG_GROUPS):
        r0 = gi * SG_GROUP_DIM
        lhs = jnp.concatenate(
            [vn[r0:r0 + SG_GROUP_DIM, n * SG_CHUNK:(n + 1) * SG_CHUNK] for n in range(nchunk)],
            axis=0)
        wT = jnp.where(causal, sgwT_ref[gi * SG_CHUNK:(gi + 1) * SG_CHUNK, :], 0.0)
        wT = wT.astype(_BF16)
        mixed = _dot(lhs, wT) + sgb_ref[gi:gi + 1, :]
        rows.append(jnp.concatenate(
            [mixed[n * SG_GROUP_DIM:(n + 1) * SG_GROUP_DIM] for n in range(nchunk)], axis=1))
    sgT_ref[...] = (uT * jnp.concatenate(rows, axis=0)).astype(_BF16)


def _even_proj(x, g, winT, qn, wuqT, kvn, wuvT, wk, sgn, sgwT, sgb, cosT, sinT):
    S = x.shape[0]
    tm = min(PROJ_TM, S)
    colT = lambda rows: pl.BlockSpec((rows, tm), lambda i: (0, i))
    consts = [g, winT, qn, wuqT, kvn, wuvT, wk, sgn, sgwT, sgb]
    return pl.pallas_call(
        _even_proj_body,
        out_shape=(jax.ShapeDtypeStruct((MLA_HEADS * HEAD_PAD, S), _BF16),
                   jax.ShapeDtypeStruct((MLA_HEADS, S, HEAD_PAD), _BF16),
                   jax.ShapeDtypeStruct((MLA_OUT, S), _BF16),
                   jax.ShapeDtypeStruct((SG_WIDTH, S), _BF16)),
        grid=(S // tm,),
        in_specs=[pl.BlockSpec((tm, D_MODEL), lambda i: (i, 0))]
        + [_const_spec(c) for c in consts]
        + [colT(HALF_ROPE), colT(HALF_ROPE)],
        out_specs=(colT(MLA_HEADS * HEAD_PAD),
                   pl.BlockSpec((MLA_HEADS, tm, HEAD_PAD), lambda i: (0, i, 0)),
                   colT(MLA_OUT), colT(SG_WIDTH)),
        compiler_params=_params(("arbitrary",)),
        name="even_proj",
    )(x, *map(_operand, consts), cosT, sinT)


ONES_ROWS = 16


ATT_LEAD = 2
ATT_UNROLL = 16


def _attn_tables(nq):
    below = [(iq, jk) for iq in range(nq) for jk in range(2 * iq)]
    diag = [(iq, 2 * iq + d) for iq in range(nq) for d in range(2)]
    rows, seqs = [], []
    for seq in (below, diag):
        if seq:
            seqs.append((len(rows) + ATT_LEAD, len(seq)))
            rows += [(0, 0)] * ATT_LEAD + seq + [seq[-1]] * ATT_LEAD
        else:
            seqs.append((0, 0))
    return tuple(jnp.asarray([r[c] for r in rows], jnp.int32) for c in range(2)), seqs


def _attn_body(iq_ref, jk_ref, qT_ref, k_ref, vT_ref, o_ref,
               s0_ref, s1_ref, c0_ref, c1_ref, p0_ref, p1_ref, a0_ref, a1_ref,
               m_ref, acc_ref, bias_ref, *, seqs):
    tk, tq = s0_ref.shape
    nq = acc_ref.shape[0]
    bufs = ((s0_ref, c0_ref, p0_ref, a0_ref), (s1_ref, c1_ref, p1_ref, a1_ref))

    @pl.when(pl.program_id(0) == 0)
    def _():
        r = lax.broadcasted_iota(jnp.int32, (tk, tq), 0)
        q = lax.broadcasted_iota(jnp.int32, (tk, tq), 1)
        bias_ref[0] = jnp.where(r <= q, 0.0, -jnp.inf)
        bias_ref[1] = jnp.where(r + tk <= q, 0.0, -jnp.inf)

    def stage_a(e, s_ref, c_ref, kind):
        lo = 0 if kind != 1 else tq // 2
        k0 = pl.multiple_of(jk_ref[e] * tk, tk)
        q0 = pl.multiple_of(iq_ref[e] * tq + lo, tq // 2)
        s = _dot(k_ref[pl.ds(k0, tk), :], qT_ref[:, pl.ds(q0, tq - lo)])
        if kind is not None:
            s = s + bias_ref[kind, :, lo:tq]
        s_ref[:, lo:tq] = s
        c_ref[:, lo:tq] = jnp.max(s, axis=0, keepdims=True)

    def stage_b(e, s_ref, c_ref, p_ref, a_ref, lo):
        iq = iq_ref[e]
        m_old = m_ref[iq, :, lo:tq]
        m_new = jnp.maximum(m_old, c_ref[:, lo:tq])
        m_ref[iq, :, lo:tq] = m_new
        a_ref[:, lo:tq] = jnp.exp2(m_old - m_new)
        p_ref[:, lo:tq] = jnp.exp2(s_ref[:, lo:tq] - m_new).astype(_BF16)

    def stage_c(e, p_ref, a_ref, lo):
        iq = iq_ref[e]
        k0 = pl.multiple_of(jk_ref[e] * tk, tk)
        v1 = jnp.concatenate([vT_ref[:, pl.ds(k0, tk)], jnp.ones((ONES_ROWS, tk), _BF16)], axis=0)
        acc_ref[iq, :, lo:tq] = (a_ref[:, lo:tq] * acc_ref[iq, :, lo:tq]
                                 + _dot(v1, p_ref[:, lo:tq]))

    def run(first, n_steps, masked):
        unroll = max(u for u in range(2, ATT_UNROLL + 1, 2) if n_steps % u == 0)
        kind = lambda d: d % 2 if masked else None
        lo = lambda d: tq // 2 if masked and d % 2 else 0
        for _, _, p_ref, a_ref in bufs:
            p_ref[...] = jnp.zeros(p_ref.shape, _BF16)
            a_ref[...] = jnp.ones(a_ref.shape, _F32)
        for d in range(ATT_LEAD):
            stage_a(first + d, *bufs[d % 2][:2], kind(d))

        def body(u, carry):
            for d in range(unroll):
                e = first + u * unroll + d
                s_ref, c_ref, p_ref, a_ref = bufs[d % 2]
                stage_c(e - ATT_LEAD, p_ref, a_ref, lo(d))
                stage_b(e, s_ref, c_ref, p_ref, a_ref, lo(d))
                stage_a(e + ATT_LEAD, s_ref, c_ref, kind(d))
            return carry

        lax.fori_loop(0, n_steps // unroll, body, 0)
        for d in range(ATT_LEAD):
            stage_c(first + n_steps - ATT_LEAD + d, *bufs[d % 2][2:], lo(d))

    acc_ref[...] = jnp.zeros(acc_ref.shape, _F32)
    m_ref[...] = jnp.full(m_ref.shape, -jnp.inf, _F32)
    for (first, n_steps), masked in zip(seqs, (False, True)):
        if n_steps:
            run(first, n_steps, masked)
    for iq in range(nq):
        acc = acc_ref[iq]
        o_ref[:, iq * tq:(iq + 1) * tq] = (
            acc[0:V_HEAD_DIM] / acc[V_HEAD_DIM:V_HEAD_DIM + 1]).astype(_BF16)


def _attention(qT, k, vT):
    S = qT.shape[1]
    tq = min(ATT_TQ, S)
    tk = tq // 2
    nq = S // tq
    tables, seqs = _attn_tables(nq)
    assert ATT_LEAD == 2 and ATT_UNROLL % 2 == 0
    grid_spec = pltpu.PrefetchScalarGridSpec(
        num_scalar_prefetch=2,
        grid=(MLA_HEADS,),
        in_specs=[pl.BlockSpec((HEAD_PAD, S), lambda h, *_: (h, 0)),
                  pl.BlockSpec((None, S, HEAD_PAD), lambda h, *_: (h, 0, 0)),
                  pl.BlockSpec((V_HEAD_DIM, S), lambda h, *_: (h, 0))],
        out_specs=pl.BlockSpec((V_HEAD_DIM, S), lambda h, *_: (h, 0)),
        scratch_shapes=[pltpu.VMEM((tk, tq), _F32), pltpu.VMEM((tk, tq), _F32),
                        pltpu.VMEM((1, tq), _F32), pltpu.VMEM((1, tq), _F32),
                        pltpu.VMEM((tk, tq), _BF16), pltpu.VMEM((tk, tq), _BF16),
                        pltpu.VMEM((1, tq), _F32), pltpu.VMEM((1, tq), _F32),
                        pltpu.VMEM((nq, 1, tq), _F32),
                        pltpu.VMEM((nq, V_HEAD_DIM + ONES_ROWS, tq), _F32),
                        pltpu.VMEM((2, tk, tq), _F32)])
    return pl.pallas_call(
        functools.partial(_attn_body, seqs=tuple(seqs)),
        out_shape=jax.ShapeDtypeStruct((MLA_OUT, S), _BF16),
        grid_spec=grid_spec,
        compiler_params=_params(("arbitrary",)),
        name="mla_attention",
    )(*tables, qT, k, vT)


CONV_HALO = 8
CONV_TC = V7X_MXU_COLS


def _conv_body(x_ref, g_ref, win_ref, cw_ref, wout_ref, o_ref, h_ref, cz_ref, by_ref):
    tm = x_ref.shape[0]
    x = x_ref[...]
    h_ref[...] = _rms_rows(x, g_ref[...]).astype(_BF16)
    W = D_MODEL

    @pl.when(pl.program_id(0) == 0)
    def _():
        cz_ref[0:CONV_HALO, :] = jnp.zeros((CONV_HALO, W), _F32)

    for j in range(W // CONV_TC):
        cols = slice(j * CONV_TC, (j + 1) * CONV_TC)
        proj = [_dot(h_ref[...], win_ref[:, part * W + j * CONV_TC:part * W + (j + 1) * CONV_TC])
                for part in range(3)]
        b_gate, cz = proj[0], proj[1] * proj[2]
        cz_ref[CONV_HALO:CONV_HALO + tm, cols] = cz
        y = cz * cw_ref[2:3, cols]
        y = y + cz_ref[CONV_HALO - 1:CONV_HALO - 1 + tm, cols] * cw_ref[1:2, cols]
        y = y + cz_ref[CONV_HALO - 2:CONV_HALO - 2 + tm, cols] * cw_ref[0:1, cols]
        cz_ref[0:CONV_HALO, cols] = cz[tm - CONV_HALO:tm]
        by_ref[:, cols] = (b_gate * y).astype(_BF16)
    o_ref[...] = x + _dot(by_ref[...], wout_ref[...])


def _conv_mixer(x, g, win, cw, wout):
    S = x.shape[0]
    tm = min(PROJ_TM, S)
    row_spec = pl.BlockSpec((tm, D_MODEL), lambda i: (i, 0))
    return pl.pallas_call(
        _conv_body,
        out_shape=jax.ShapeDtypeStruct((S, D_MODEL), _F32),
        grid=(S // tm,),
        in_specs=[row_spec, _const_spec(g), _const_spec(win), _const_spec(cw),
                  _const_spec(wout)],
        out_specs=row_spec,
        scratch_shapes=[pltpu.VMEM((tm, D_MODEL), _BF16),
                        pltpu.VMEM((tm + CONV_HALO, D_MODEL), _F32),
                        pltpu.VMEM((tm, D_MODEL), _BF16)],
        compiler_params=_params(("arbitrary",)),
        name="conv_mixer",
    )(x, *map(_operand, (g, win, cw, wout)))


def _prep_even(w_in, w_uq, w_ukv, sg_w):
    n = w_in.shape[0]
    winT = jnp.swapaxes(w_in, 1, 2).astype(_BF16)
    wq = w_uq.reshape(n, Q_LORA_RANK, MLA_HEADS, QK_HEAD_DIM)
    wq = jnp.pad(wq, ((0, 0), (0, 0), (0, 0), (0, HEAD_PAD - QK_HEAD_DIM)))
    wuqT = jnp.swapaxes(wq.reshape(n, Q_LORA_RANK, MLA_HEADS * HEAD_PAD), 1, 2).astype(_BF16)
    wkv = w_ukv.reshape(n, KV_LORA_RANK, MLA_HEADS, QK_NOPE_DIM + V_HEAD_DIM)
    wuvT = jnp.swapaxes(wkv[..., QK_NOPE_DIM:].reshape(n, KV_LORA_RANK, MLA_OUT), 1, 2)
    wk_nope = jnp.pad(wkv[..., :QK_NOPE_DIM],
                      ((0, 0), (0, 0), (0, 0), (0, HEAD_PAD - QK_NOPE_DIM)))
    place = jnp.pad(jnp.eye(QK_ROPE_DIM, dtype=_F32),
                    ((0, 0), (QK_NOPE_DIM, HEAD_PAD - QK_HEAD_DIM)))
    place = jnp.broadcast_to(place[None, :, None, :], (n, QK_ROPE_DIM, MLA_HEADS, HEAD_PAD))
    wk = jnp.concatenate([wk_nope, place], axis=1).reshape(
        n, KV_LORA_RANK + QK_ROPE_DIM, MLA_HEADS * HEAD_PAD).astype(_BF16)
    sgwT = jnp.swapaxes(sg_w, 2, 3).reshape(n, SG_GROUPS * SG_CHUNK, SG_CHUNK)
    return winT, wuqT, wuvT.astype(_BF16), wk, sgwT


def kernel(x, positions, ffn_pre_norm, ffn_pre_w_gate, ffn_pre_w_up, ffn_pre_w_down, mix_norm, ffn_post_norm, ffn_post_w_gate, ffn_post_w_up, ffn_post_w_down, even_w_in, q_norm, w_uq, kv_norm, w_ukv, sg_norm, sg_w, sg_b, even_w_out, conv_w_in, conv_w, conv_w_out, final_norm):
    B, S, _ = x.shape
    assert B == 1 and S % SG_CHUNK == 0
    xs = x[0]
    inv_freq = ROPE_THETA ** (-jnp.arange(0, QK_ROPE_DIM, 2, dtype=_F32) / QK_ROPE_DIM)
    ang = positions[0].astype(_F32)[None, :] * inv_freq[:, None]
    cosT, sinT = jnp.cos(ang), jnp.sin(ang)
    bf = lambda w: w.astype(_BF16)
    rows = lambda v: v[:, None, :]
    cols = lambda v: v[:, :, None]

    pre_w = (ffn_pre_w_gate, ffn_pre_w_up, ffn_pre_w_down)
    post_w = (ffn_post_w_gate, ffn_post_w_up, ffn_post_w_down)
    pre_g, post_g, mix_g = rows(ffn_pre_norm), rows(ffn_post_norm), rows(mix_norm)
    w_bf16 = tuple(bf(w[0]) for w in pre_w)
    winT, wuqT, wuvT, wk, sgwT = _prep_even(even_w_in, w_uq, w_ukv, sg_w)
    qn, kvn, sgn, wo = cols(q_norm), cols(kv_norm), cols(sg_norm), bf(even_w_out)
    conv = (bf(conv_w_in), conv_w, bf(conv_w_out))

    for layer in range(DEPTH):
        xs, w_bf16 = _ffn(xs, _Slab(pre_g, layer), *w_bf16, nxt=(layer, *post_w))
        mix = None
        if layer % 2 == 0:
            e = layer // 2
            sl = lambda w: _Slab(w, e)
            qT, k, vT, sgT = _even_proj(
                xs, _Slab(mix_g, layer), sl(winT), sl(qn), sl(wuqT), sl(kvn), sl(wuvT), sl(wk),
                sl(sgn), sl(sgwT), sl(sg_b), cosT, sinT)
            attnT = _attention(qT, k, vT)
            mix = (attnT, sgT, _Slab(wo, e, 0, 2), _Slab(wo, e, 1, 2))
        else:
            o = layer // 2
            xs = _conv_mixer(xs, _Slab(mix_g, layer), *(_Slab(w, o) for w in conv))
        last = layer == DEPTH - 1
        xs, w_bf16 = _ffn(xs, _Slab(post_g, layer), *w_bf16, mix=mix,
                          final_g=final_norm.reshape(1, -1) if last else None,
                          nxt=None if last else (layer + 1, *pre_w))
    return xs[None]
```

```python
import functools
import math
from typing import NamedTuple

import jax
import jax.numpy as jnp
from jax import lax
from jax.experimental import pallas as pl
from jax.experimental.pallas import tpu as pltpu

D_MODEL = 1024
DEPTH = 4
MLA_HEADS = 8
QK_NOPE_DIM = 64
QK_ROPE_DIM = 32
V_HEAD_DIM = 64
Q_LORA_RANK = 384
KV_LORA_RANK = 256
ROPE_THETA = 10000.0
SG_GROUPS = 8
SG_GROUP_DIM = 64
SG_WIDTH = SG_GROUPS * SG_GROUP_DIM
SG_CHUNK = 128
CONV_K = 3
D_FF = 2816
NORM_EPS = 1e-6
QK_HEAD_DIM = QK_NOPE_DIM + QK_ROPE_DIM
MLA_OUT = MLA_HEADS * V_HEAD_DIM
EVEN_IN = Q_LORA_RANK + KV_LORA_RANK + QK_ROPE_DIM + 2 * SG_WIDTH
HALF_ROPE = QK_ROPE_DIM // 2

V7X_LANES = 128
V7X_MXU_COLS = 256
V7X_VMEM_BYTES = 64 * 1024 * 1024

HEAD_PAD = V7X_LANES
FFN_TM = 1024
FFN_TF = V7X_MXU_COLS
FFN_CAST_BANDS = 16
PROJ_TM = 1024
ATT_TQ = 1024
VMEM_LIMIT = 56 * 1024 * 1024

_NT = (((1,), (1,)), ((), ()))
_TN = (((0,), (0,)), ((), ()))
_F32 = jnp.float32
_BF16 = jnp.bfloat16


def _dot(a, b, dims=None):
    if dims is None:
        return jnp.dot(a, b, preferred_element_type=_F32)
    return lax.dot_general(a, b, dims, preferred_element_type=_F32)


def _rms_rows(x, g):
    ms = jnp.mean(x * x, axis=-1, keepdims=True)
    return x * lax.rsqrt(ms + NORM_EPS) * g


def _rms_cols(xT, g):
    ms = jnp.mean(xT * xT, axis=0, keepdims=True)
    return xT * lax.rsqrt(ms + NORM_EPS) * g


class _Slab(NamedTuple):
    stack: jax.Array
    index: int
    part: int = 0
    parts: int = 1


def _operand(p):
    return p.stack if isinstance(p, _Slab) else p


def _const_spec(p):
    if isinstance(p, _Slab):
        _, rows, cols = p.stack.shape
        return pl.BlockSpec((None, rows // p.parts, cols), lambda *_: (p.index, p.part, 0),
                            pipeline_mode=pl.Buffered(1))
    n = p.ndim
    return pl.BlockSpec(p.shape, lambda *_: (0,) * n, pipeline_mode=pl.Buffered(1))


def _params(semantics, flags=None):
    return pltpu.CompilerParams(dimension_semantics=semantics,
                                vmem_limit_bytes=VMEM_LIMIT, flags=flags)


def _ffn_body(has_mix, has_final, has_next, *refs):
    it = iter(refs)
    x_ref = next(it)
    if has_mix:
        attn_ref, sg_ref, woa_ref, wos_ref = next(it), next(it), next(it), next(it)
    g_ref, wg_ref, wu_ref, wd_ref = next(it), next(it), next(it), next(it)
    gf_ref = next(it) if has_final else None
    next_f32 = [next(it) for _ in range(3)] if has_next else []
    o_ref = next(it)
    next_bf16 = [next(it) for _ in range(3)] if has_next else []
    h_ref, a_ref = next(it), next(it)

    for src_ref, dst_ref in zip(next_f32, next_bf16):
        dst_ref[...] = src_ref[...].astype(_BF16)

    x = x_ref[...]
    if has_mix:
        x = x + _dot(attn_ref[...], woa_ref[...], _TN) + _dot(sg_ref[...], wos_ref[...], _TN)
    h_ref[...] = _rms_rows(x, g_ref[...]).astype(_BF16)
    for j in range(D_FF // FFN_TF):
        cols = slice(j * FFN_TF, (j + 1) * FFN_TF)
        gate = _dot(h_ref[...], wg_ref[:, cols])
        up = _dot(h_ref[...], wu_ref[:, cols])
        a_ref[:, cols] = (gate * jax.nn.sigmoid(gate) * up).astype(_BF16)
    out = x + 0.5 * _dot(a_ref[...], wd_ref[...])
    if has_final:
        out = _rms_rows(out, gf_ref[...])
    o_ref[...] = out


def _ffn(x, g, wg, wu, wd, mix=None, final_g=None, nxt=None):
    S = x.shape[0]
    tm = min(FFN_TM, S)
    steps = S // tm
    row_spec = pl.BlockSpec((tm, D_MODEL), lambda i: (i, 0))
    args, specs = [x], [row_spec]
    out_shape, out_specs = [jax.ShapeDtypeStruct((S, D_MODEL), _F32)], [row_spec]
    if mix is not None:
        attnT, sgT, woa, wos = mix
        colT = pl.BlockSpec((MLA_OUT, tm), lambda i: (0, i))
        args += [attnT, sgT, woa, wos]
        specs += [colT, colT, _const_spec(woa), _const_spec(wos)]
    args += [g, wg, wu, wd]
    specs += [_const_spec(g), _const_spec(wg), _const_spec(wu), _const_spec(wd)]
    if final_g is not None:
        args.append(final_g)
        specs.append(_const_spec(final_g))
    if nxt is not None:
        layer, *stacks = nxt
        bands = math.gcd(steps, FFN_CAST_BANDS)
        for w in stacks:
            _, rows, cols = w.shape
            args.append(w)
            specs.append(pl.BlockSpec((None, rows // bands, cols),
                                      lambda i: (layer, i * bands // steps, 0)))
            out_shape.append(jax.ShapeDtypeStruct((rows, cols), _BF16))
            out_specs.append(pl.BlockSpec((rows // bands, cols),
                                          lambda i: (i * bands // steps, 0)))
    outs = pl.pallas_call(
        functools.partial(_ffn_body, mix is not None, final_g is not None, nxt is not None),
        out_shape=tuple(out_shape),
        grid=(steps,),
        in_specs=specs,
        out_specs=tuple(out_specs),
        scratch_shapes=[pltpu.VMEM((tm, D_MODEL), _BF16), pltpu.VMEM((tm, D_FF), _BF16)],
        compiler_params=_params(("arbitrary",)),
        name="ffn_mix" if mix is not None else ("ffn_final" if final_g is not None else "ffn"),
    )(*map(_operand, args))
    return outs[0], tuple(outs[1:])


def _even_proj_body(x_ref, g_ref, winT_ref, qn_ref, wuqT_ref, kvn_ref, wuvT_ref,
                    wk_ref, sgn_ref, sgwT_ref, sgb_ref, cos_ref, sin_ref,
                    qT_ref, k_ref, vT_ref, sgT_ref):
    tm = x_ref.shape[0]
    h = _rms_rows(x_ref[...], g_ref[...]).astype(_BF16)
    projT = _dot(winT_ref[...], h, _NT)
    o_kv = Q_LORA_RANK
    o_kr = o_kv + KV_LORA_RANK
    o_z = o_kr + QK_ROPE_DIM
    cos = cos_ref[...]
    sin = sin_ref[...]

    cqn = _rms_cols(projT[0:o_kv], qn_ref[...]).astype(_BF16)
    qT = _dot(wuqT_ref[...], cqn) * (QK_HEAD_DIM ** -0.5 * math.log2(math.e))
    pieces = []
    for hd in range(MLA_HEADS):
        b = hd * HEAD_PAD
        t1 = qT[b + QK_NOPE_DIM:b + QK_NOPE_DIM + HALF_ROPE]
        t2 = qT[b + QK_NOPE_DIM + HALF_ROPE:b + QK_HEAD_DIM]
        pieces += [qT[b:b + QK_NOPE_DIM], t1 * cos - t2 * sin, t1 * sin + t2 * cos,
                   qT[b + QK_HEAD_DIM:b + HEAD_PAD]]
    qT_ref[...] = jnp.concatenate(pieces, axis=0).astype(_BF16)

    ckvn = _rms_cols(projT[o_kv:o_kr], kvn_ref[...]).astype(_BF16)
    vT_ref[...] = _dot(wuvT_ref[...], ckvn).astype(_BF16)
    k1 = projT[o_kr:o_kr + HALF_ROPE]
    k2 = projT[o_kr + HALF_ROPE:o_z]
    kro = jnp.concatenate([k1 * cos - k2 * sin, k1 * sin + k2 * cos], axis=0)
    lhsT = jnp.concatenate([ckvn, kro.astype(_BF16)], axis=0)
    k_all = _dot(lhsT, wk_ref[...], _TN).astype(_BF16)
    for hd in range(MLA_HEADS):
        k_ref[hd] = k_all[:, hd * HEAD_PAD:(hd + 1) * HEAD_PAD]

    z = projT[o_z:EVEN_IN]
    z = 0.5 * z * (1.0 + lax.erf(z * (2.0 ** -0.5)))
    uT = z[0:SG_WIDTH]
    vn = _rms_cols(z[SG_WIDTH:2 * SG_WIDTH], sgn_ref[...]).astype(_BF16)
    nchunk = tm // SG_CHUNK
    s_idx = lax.broadcasted_iota(jnp.int32, (SG_CHUNK, SG_CHUNK), 0)
    t_idx = lax.broadcasted_iota(jnp.int32, (SG_CHUNK, SG_CHUNK), 1)
    causal = s_idx <= t_idx
    rows = []
    for gi in range(SG_GROUPS):
        r0 = gi * SG_GROUP_DIM
        lhs = jnp.concatenate(
            [vn[r0:r0 + SG_GROUP_DIM, n * SG_CHUNK:(n + 1) * SG_CHUNK] for n in range(nchunk)],
            axis=0)
        wT = jnp.where(causal, sgwT_ref[gi * SG_CHUNK:(gi + 1) * SG_CHUNK, :], 0.0)
        wT = wT.astype(_BF16)
        mixed = _dot(lhs, wT) + sgb_ref[gi:gi + 1, :]
        rows.append(jnp.concatenate(
            [mixed[n * SG_GROUP_DIM:(n + 1) * SG_GROUP_DIM] for n in range(nchunk)], axis=1))
    sgT_ref[...] = (uT * jnp.concatenate(rows, axis=0)).astype(_BF16)


def _even_proj(x, g, winT, qn, wuqT, kvn, wuvT, wk, sgn, sgwT, sgb, cosT, sinT):
    S = x.shape[0]
    tm = min(PROJ_TM, S)
    colT = lambda rows: pl.BlockSpec((rows, tm), lambda i: (0, i))
    consts = [g, winT, qn, wuqT, kvn, wuvT, wk, sgn, sgwT, sgb]
    return pl.pallas_call(
        _even_proj_body,
        out_shape=(jax.ShapeDtypeStruct((MLA_HEADS * HEAD_PAD, S), _BF16),
                   jax.ShapeDtypeStruct((MLA_HEADS, S, HEAD_PAD), _BF16),
                   jax.ShapeDtypeStruct((MLA_OUT, S), _BF16),
                   jax.ShapeDtypeStruct((SG_WIDTH, S), _BF16)),
        grid=(S // tm,),
        in_specs=[pl.BlockSpec((tm, D_MODEL), lambda i: (i, 0))]
        + [_const_spec(c) for c in consts]
        + [colT(HALF_ROPE), colT(HALF_ROPE)],
        out_specs=(colT(MLA_HEADS * HEAD_PAD),
                   pl.BlockSpec((MLA_HEADS, tm, HEAD_PAD), lambda i: (0, i, 0)),
                   colT(MLA_OUT), colT(SG_WIDTH)),
        compiler_params=_params(("arbitrary",)),
        name="even_proj",
    )(x, *map(_operand, consts), cosT, sinT)


ONES_ROWS = 16


ATT_LEAD = 2
ATT_UNROLL = 16


def _attn_tables(nq):
    below = [(iq, jk) for iq in range(nq) for jk in range(2 * iq)]
    diag = [(iq, 2 * iq + d) for iq in range(nq) for d in range(2)]
    rows, seqs = [], []
    for seq in (below, diag):
        if seq:
            seqs.append((len(rows) + ATT_LEAD, len(seq)))
            rows += [(0, 0)] * ATT_LEAD + seq + [seq[-1]] * ATT_LEAD
        else:
            seqs.append((0, 0))
    return tuple(jnp.asarray([r[c] for r in rows], jnp.int32) for c in range(2)), seqs


def _attn_body(iq_ref, jk_ref, qT_ref, k_ref, vT_ref, o_ref,
               s0_ref, s1_ref, c0_ref, c1_ref, p0_ref, p1_ref, a0_ref, a1_ref,
               m_ref, acc_ref, bias_ref, *, seqs):
    tk, tq = s0_ref.shape
    nq = acc_ref.shape[0]
    bufs = ((s0_ref, c0_ref, p0_ref, a0_ref), (s1_ref, c1_ref, p1_ref, a1_ref))

    @pl.when(pl.program_id(0) == 0)
    def _():
        r = lax.broadcasted_iota(jnp.int32, (tk, tq), 0)
        q = lax.broadcasted_iota(jnp.int32, (tk, tq), 1)
        bias_ref[0] = jnp.where(r <= q, 0.0, -jnp.inf)
        bias_ref[1] = jnp.where(r + tk <= q, 0.0, -jnp.inf)

    def stage_a(e, s_ref, c_ref, kind):
        lo = 0 if kind != 1 else tq // 2
        k0 = pl.multiple_of(jk_ref[e] * tk, tk)
        q0 = pl.multiple_of(iq_ref[e] * tq + lo, tq // 2)
        s = _dot(k_ref[pl.ds(k0, tk), :], qT_ref[:, pl.ds(q0, tq - lo)])
        if kind == 1:
            s = s + bias_ref[1, :, lo:tq]
        elif kind == 0:
            s = jnp.concatenate([s[:, :tq // 2] + bias_ref[0, :, :tq // 2], s[:, tq // 2:]], axis=1)
        s_ref[:, lo:tq] = s
        c_ref[:, lo:tq] = jnp.max(s, axis=0, keepdims=True)

    def stage_b(e, s_ref, c_ref, p_ref, a_ref, lo):
        iq = iq_ref[e]
        m_old = m_ref[iq, :, lo:tq]
        m_new = jnp.maximum(m_old, c_ref[:, lo:tq])
        m_ref[iq, :, lo:tq] = m_new
        a_ref[:, lo:tq] = jnp.exp2(m_old - m_new)
        p_ref[:, lo:tq] = jnp.exp2(s_ref[:, lo:tq] - m_new).astype(_BF16)

    def stage_c(e, p_ref, a_ref, lo):
        iq = iq_ref[e]
        k0 = pl.multiple_of(jk_ref[e] * tk, tk)
        v1 = jnp.concatenate([vT_ref[:, pl.ds(k0, tk)], jnp.ones((ONES_ROWS, tk), _BF16)], axis=0)
        acc_ref[iq, :, lo:tq] = (a_ref[:, lo:tq] * acc_ref[iq, :, lo:tq]
                                 + _dot(v1, p_ref[:, lo:tq]))

    def run(first, n_steps, masked):
        unroll = max(u for u in range(2, ATT_UNROLL + 1, 2) if n_steps % u == 0)
        kind = lambda d: d % 2 if masked else None
        lo = lambda d: tq // 2 if masked and d % 2 else 0
        for _, _, p_ref, a_ref in bufs:
            p_ref[...] = jnp.zeros(p_ref.shape, _BF16)
            a_ref[...] = jnp.ones(a_ref.shape, _F32)
        for d in range(ATT_LEAD):
            stage_a(first + d, *bufs[d % 2][:2], kind(d))

        def body(u, carry):
            for d in range(unroll):
                e = first + u * unroll + d
                s_ref, c_ref, p_ref, a_ref = bufs[d % 2]
                stage_c(e - ATT_LEAD, p_ref, a_ref, lo(d))
                stage_b(e, s_ref, c_ref, p_ref, a_ref, lo(d))
                stage_a(e + ATT_LEAD, s_ref, c_ref, kind(d))
            return carry

        lax.fori_loop(0, n_steps // unroll, body, 0)
        for d in range(ATT_LEAD):
            stage_c(first + n_steps - ATT_LEAD + d, *bufs[d % 2][2:], lo(d))

    acc_ref[...] = jnp.zeros(acc_ref.shape, _F32)
    m_ref[...] = jnp.full(m_ref.shape, -jnp.inf, _F32)
    for (first, n_steps), masked in zip(seqs, (False, True)):
        if n_steps:
            run(first, n_steps, masked)
    for iq in range(nq):
        acc = acc_ref[iq]
        o_ref[:, iq * tq:(iq + 1) * tq] = (
            acc[0:V_HEAD_DIM] / acc[V_HEAD_DIM:V_HEAD_DIM + 1]).astype(_BF16)


def _attention(qT, k, vT):
    S = qT.shape[1]
    tq = min(ATT_TQ, S)
    tk = tq // 2
    nq = S // tq
    tables, seqs = _attn_tables(nq)
    assert ATT_LEAD == 2 and ATT_UNROLL % 2 == 0
    grid_spec = pltpu.PrefetchScalarGridSpec(
        num_scalar_prefetch=2,
        grid=(MLA_HEADS,),
        in_specs=[pl.BlockSpec((HEAD_PAD, S), lambda h, *_: (h, 0)),
                  pl.BlockSpec((None, S, HEAD_PAD), lambda h, *_: (h, 0, 0)),
                  pl.BlockSpec((V_HEAD_DIM, S), lambda h, *_: (h, 0))],
        out_specs=pl.BlockSpec((V_HEAD_DIM, S), lambda h, *_: (h, 0)),
        scratch_shapes=[pltpu.VMEM((tk, tq), _F32), pltpu.VMEM((tk, tq), _F32),
                        pltpu.VMEM((1, tq), _F32), pltpu.VMEM((1, tq), _F32),
                        pltpu.VMEM((tk, tq), _BF16), pltpu.VMEM((tk, tq), _BF16),
                        pltpu.VMEM((1, tq), _F32), pltpu.VMEM((1, tq), _F32),
                        pltpu.VMEM((nq, 1, tq), _F32),
                        pltpu.VMEM((nq, V_HEAD_DIM + ONES_ROWS, tq), _F32),
                        pltpu.VMEM((2, tk, tq), _F32)])
    return pl.pallas_call(
        functools.partial(_attn_body, seqs=tuple(seqs)),
        out_shape=jax.ShapeDtypeStruct((MLA_OUT, S), _BF16),
        grid_spec=grid_spec,
        compiler_params=_params(("arbitrary",)),
        name="mla_attention",
    )(*tables, qT, k, vT)


CONV_HALO = 8
CONV_TC = V7X_MXU_COLS


def _conv_body(x_ref, g_ref, win_ref, cw_ref, wout_ref, o_ref, h_ref, cz_ref, by_ref):
    tm = x_ref.shape[0]
    x = x_ref[...]
    h_ref[...] = _rms_rows(x, g_ref[...]).astype(_BF16)
    W = D_MODEL

    @pl.when(pl.program_id(0) == 0)
    def _():
        cz_ref[0:CONV_HALO, :] = jnp.zeros((CONV_HALO, W), _F32)

    for j in range(W // CONV_TC):
        cols = slice(j * CONV_TC, (j + 1) * CONV_TC)
        proj = [_dot(h_ref[...], win_ref[:, part * W + j * CONV_TC:part * W + (j + 1) * CONV_TC])
                for part in range(3)]
        b_gate, cz = proj[0], proj[1] * proj[2]
        cz_ref[CONV_HALO:CONV_HALO + tm, cols] = cz
        y = cz * cw_ref[2:3, cols]
        y = y + cz_ref[CONV_HALO - 1:CONV_HALO - 1 + tm, cols] * cw_ref[1:2, cols]
        y = y + cz_ref[CONV_HALO - 2:CONV_HALO - 2 + tm, cols] * cw_ref[0:1, cols]
        cz_ref[0:CONV_HALO, cols] = cz[tm - CONV_HALO:tm]
        by_ref[:, cols] = (b_gate * y).astype(_BF16)
    o_ref[...] = x + _dot(by_ref[...], wout_ref[...])


def _conv_mixer(x, g, win, cw, wout):
    S = x.shape[0]
    tm = min(PROJ_TM, S)
    row_spec = pl.BlockSpec((tm, D_MODEL), lambda i: (i, 0))
    return pl.pallas_call(
        _conv_body,
        out_shape=jax.ShapeDtypeStruct((S, D_MODEL), _F32),
        grid=(S // tm,),
        in_specs=[row_spec, _const_spec(g), _const_spec(win), _const_spec(cw),
                  _const_spec(wout)],
        out_specs=row_spec,
        scratch_shapes=[pltpu.VMEM((tm, D_MODEL), _BF16),
                        pltpu.VMEM((tm + CONV_HALO, D_MODEL), _F32),
                        pltpu.VMEM((tm, D_MODEL), _BF16)],
        compiler_params=_params(("arbitrary",)),
        name="conv_mixer",
    )(x, *map(_operand, (g, win, cw, wout)))


def _prep_even(w_in, w_uq, w_ukv, sg_w):
    n = w_in.shape[0]
    winT = jnp.swapaxes(w_in, 1, 2).astype(_BF16)
    wq = w_uq.reshape(n, Q_LORA_RANK, MLA_HEADS, QK_HEAD_DIM)
    wq = jnp.pad(wq, ((0, 0), (0, 0), (0, 0), (0, HEAD_PAD - QK_HEAD_DIM)))
    wuqT = jnp.swapaxes(wq.reshape(n, Q_LORA_RANK, MLA_HEADS * HEAD_PAD), 1, 2).astype(_BF16)
    wkv = w_ukv.reshape(n, KV_LORA_RANK, MLA_HEADS, QK_NOPE_DIM + V_HEAD_DIM)
    wuvT = jnp.swapaxes(wkv[..., QK_NOPE_DIM:].reshape(n, KV_LORA_RANK, MLA_OUT), 1, 2)
    wk_nope = jnp.pad(wkv[..., :QK_NOPE_DIM],
                      ((0, 0), (0, 0), (0, 0), (0, HEAD_PAD - QK_NOPE_DIM)))
    place = jnp.pad(jnp.eye(QK_ROPE_DIM, dtype=_F32),
                    ((0, 0), (QK_NOPE_DIM, HEAD_PAD - QK_HEAD_DIM)))
    place = jnp.broadcast_to(place[None, :, None, :], (n, QK_ROPE_DIM, MLA_HEADS, HEAD_PAD))
    wk = jnp.concatenate([wk_nope, place], axis=1).reshape(
        n, KV_LORA_RANK + QK_ROPE_DIM, MLA_HEADS * HEAD_PAD).astype(_BF16)
    sgwT = jnp.swapaxes(sg_w, 2, 3).reshape(n, SG_GROUPS * SG_CHUNK, SG_CHUNK)
    return winT, wuqT, wuvT.astype(_BF16), wk, sgwT


def kernel(x, positions, ffn_pre_norm, ffn_pre_w_gate, ffn_pre_w_up, ffn_pre_w_down, mix_norm, ffn_post_norm, ffn_post_w_gate, ffn_post_w_up, ffn_post_w_down, even_w_in, q_norm, w_uq, kv_norm, w_ukv, sg_norm, sg_w, sg_b, even_w_out, conv_w_in, conv_w, conv_w_out, final_norm):
    B, S, _ = x.shape
    assert B == 1 and S % SG_CHUNK == 0
    xs = x[0]
    inv_freq = ROPE_THETA ** (-jnp.arange(0, QK_ROPE_DIM, 2, dtype=_F32) / QK_ROPE_DIM)
    ang = positions[0].astype(_F32)[None, :] * inv_freq[:, None]
    cosT, sinT = jnp.cos(ang), jnp.sin(ang)
    bf = lambda w: w.astype(_BF16)
    rows = lambda v: v[:, None, :]
    cols = lambda v: v[:, :, None]

    pre_w = (ffn_pre_w_gate, ffn_pre_w_up, ffn_pre_w_down)
    post_w = (ffn_post_w_gate, ffn_post_w_up, ffn_post_w_down)
    pre_g, post_g, mix_g = rows(ffn_pre_norm), rows(ffn_post_norm), rows(mix_norm)
    w_bf16 = tuple(bf(w[0]) for w in pre_w)
    winT, wuqT, wuvT, wk, sgwT = _prep_even(even_w_in, w_uq, w_ukv, sg_w)
    qn, kvn, sgn, wo = cols(q_norm), cols(kv_norm), cols(sg_norm), bf(even_w_out)
    conv = (bf(conv_w_in), conv_w, bf(conv_w_out))

    for layer in range(DEPTH):
        xs, w_bf16 = _ffn(xs, _Slab(pre_g, layer), *w_bf16, nxt=(layer, *post_w))
        mix = None
        if layer % 2 == 0:
            e = layer // 2
            sl = lambda w: _Slab(w, e)
            qT, k, vT, sgT = _even_proj(
                xs, _Slab(mix_g, layer), sl(winT), sl(qn), sl(wuqT), sl(kvn), sl(wuvT), sl(wk),
                sl(sgn), sl(sgwT), sl(sg_b), cosT, sinT)
            attnT = _attention(qT, k, vT)
            mix = (attnT, sgT, _Slab(wo, e, 0, 2), _Slab(wo, e, 1, 2))
        else:
            o = layer // 2
            xs = _conv_mixer(xs, _Slab(mix_g, layer), *(_Slab(w, o) for w in conv))
        last = layer == DEPTH - 1
        xs, w_bf16 = _ffn(xs, _Slab(post_g, layer), *w_bf16, mix=mix,
                          final_g=final_norm.reshape(1, -1) if last else None,
                          nxt=None if last else (layer + 1, *pre_w))
    return xs[None]
```
